```python
import jax, jax.numpy as jnp
from jax import lax
import numpy as np

D_MODEL = 1024
BATCH = 16
SEQ = 2048
DEPTH = 2

POOL_GROUPS = 4
POOL_GROUP_DIM = 64
POOL_WINDOWS = (2, 4, 8, 16)
D_POOL = POOL_GROUPS * POOL_GROUP_DIM
D_CONV = 256
CONV_WIDTH = 31
N_HEADS = 8
N_KV_HEADS = 2
HEAD_DIM = 64
D_ATTN = N_HEADS * HEAD_DIM
D_KV = N_KV_HEADS * HEAD_DIM
IDX_HEADS = 8
IDX_DIM = 64
TOPK_MAX = 256
Q_BLOCK = 128
ROPE_THETA = 10000.0
N_BRANCH = 3
IN_SIZES = (D_POOL, 2 * D_CONV, D_ATTN, D_KV, D_KV, IDX_HEADS * IDX_DIM, IDX_DIM, IDX_HEADS, N_BRANCH * D_MODEL)
D_IN = sum(IN_SIZES)
D_FF = 2816
FFN_CONV_WIDTH = 3
LN_EPS = 1e-5
DEEPNORM_ALPHA = (2 * DEPTH) ** 0.25
DEEPNORM_BETA = (8 * DEPTH) ** -0.25

kernel_name = "hybrid_pool_conformer_dsa_gated_deepnorm"


def layer_norm(x, g, b):
    xf = x.astype(jnp.float32)
    mu = jnp.mean(xf, axis=-1, keepdims=True)
    var = jnp.mean(jnp.square(xf - mu), axis=-1, keepdims=True)
    y = (xf - mu) * lax.rsqrt(var + LN_EPS) * g.astype(jnp.float32) + b.astype(jnp.float32)
    return y.astype(x.dtype)


def rope(x, positions):
    half = x.shape[-1] // 2
    inv_freq = ROPE_THETA ** (-jnp.arange(half, dtype=jnp.float32) / half)
    ang = positions.astype(jnp.float32)[..., None] * inv_freq
    cos = jnp.cos(ang)[:, :, None, :]
    sin = jnp.sin(ang)[:, :, None, :]
    xf = x.astype(jnp.float32)
    x1, x2 = xf[..., :half], xf[..., half:]
    return jnp.concatenate([x1 * cos - x2 * sin, x2 * cos + x1 * sin], axis=-1).astype(x.dtype)


def causal_dwconv(x, w, b):
    k, c = w.shape
    y = lax.conv_general_dilated(
        x, w[:, None, :].astype(x.dtype), window_strides=(1,), padding=[(k - 1, 0)],
        dimension_numbers=('NWC', 'WIO', 'NWC'), feature_group_count=c)
    return y + b


def split_cols(z, sizes):
    out, off = [], 0
    for s in sizes:
        out.append(z[..., off:off + s])
        off += s
    return out


def pool_mixer(u, pool_w, pool_scale):
    B, L, _ = u.shape
    uf = u.astype(jnp.float32).reshape(B, L, POOL_GROUPS, POOL_GROUP_DIM)
    cs0 = jnp.pad(jnp.cumsum(uf, axis=1), ((0, 0), (1, 0), (0, 0), (0, 0)))
    t = jnp.arange(L)
    pooled = []
    for g, win in enumerate(POOL_WINDOWS):
        start = jnp.maximum(t + 1 - win, 0)
        s = cs0[:, 1:, g] - cs0[:, start, g]
        cnt = (t + 1 - start).astype(jnp.float32)[None, :, None]
        pooled.append(s / cnt)
    mixed = jnp.stack(pooled, axis=2) - uf
    y = jnp.einsum('blgc,gcd->blgd', mixed, pool_w.astype(jnp.float32))
    return (y.reshape(B, L, D_POOL) * pool_scale.astype(jnp.float32)).astype(u.dtype)


def conformer_conv(u, dw_w, dw_b, ln_g, ln_b, w_pw):
    a, gate = jnp.split(u, 2, axis=-1)
    h = a * jax.nn.sigmoid(gate)
    h = causal_dwconv(h, dw_w, dw_b)
    h = jax.nn.silu(layer_norm(h, ln_g, ln_b))
    return h @ w_pw


def dsa_attention(q, k, v, q_idx, k_idx, w_idx):
    B, L = q.shape[0], q.shape[1]
    topk = min(TOPK_MAX, L // 4)
    n_blk = L // Q_BLOCK
    group = N_HEADS // N_KV_HEADS
    key_pos = jnp.arange(L)
    k_idx_f = k_idx.astype(jnp.float32)

    def block(i):
        q0 = i * Q_BLOCK
        qpos = q0 + jnp.arange(Q_BLOCK)
        qb = lax.dynamic_slice_in_dim(q, q0, Q_BLOCK, axis=1)
        qib = lax.dynamic_slice_in_dim(q_idx, q0, Q_BLOCK, axis=1).astype(jnp.float32)
        wib = lax.dynamic_slice_in_dim(w_idx, q0, Q_BLOCK, axis=1).astype(jnp.float32)
        logits = jnp.einsum('bqhd,bsd->bqsh', qib, k_idx_f)
        score = jnp.einsum('bqsh,bqh->bqs', jax.nn.relu(logits), wib)
        causal = key_pos[None, :] <= qpos[:, None]
        score = jnp.where(causal[None], score, -jnp.inf)
        _, idx = lax.top_k(score, topk)
        valid = idx <= qpos[None, :, None]
        k_sel = jax.vmap(lambda kk, ii: kk[ii])(k, idx)
        v_sel = jax.vmap(lambda vv, ii: vv[ii])(v, idx)
        qg = qb.reshape(B, Q_BLOCK, N_KV_HEADS, group, HEAD_DIM).astype(jnp.float32)
        s = jnp.einsum('bqngd,bqknd->bngqk', qg, k_sel.astype(jnp.float32)) * (HEAD_DIM ** -0.5)
        s = jnp.where(valid[:, None, None], s, -jnp.inf)
        p = jax.nn.softmax(s, axis=-1)
        o = jnp.einsum('bngqk,bqknd->bqngd', p, v_sel.astype(jnp.float32))
        return o.reshape(B, Q_BLOCK, D_ATTN).astype(q.dtype)

    out = lax.map(block, jnp.arange(n_blk))
    return out.transpose(1, 0, 2, 3).reshape(B, L, D_ATTN)


def token_mix(h, positions, w_in, b_in, pool_w, pool_scale, w_pool_out, conv_dw_w, conv_dw_b,
              conv_ln_g, conv_ln_b, w_conv_out, w_attn_out, w_o):
    B, L, _ = h.shape
    z = h @ w_in + b_in
    u_pool, u_conv, q, k, v, qi, ki, wi, gates = split_cols(z, IN_SIZES)
    q = rope(q.reshape(B, L, N_HEADS, HEAD_DIM), positions)
    k = rope(k.reshape(B, L, N_KV_HEADS, HEAD_DIM), positions)
    v = v.reshape(B, L, N_KV_HEADS, HEAD_DIM)
    qi = rope(qi.reshape(B, L, IDX_HEADS, IDX_DIM), positions) * (IDX_DIM ** -0.5)
    ki = rope(ki[:, :, None, :], positions)[:, :, 0, :]
    wi = wi * (IDX_HEADS ** -0.5)
    y_pool = pool_mixer(u_pool, pool_w, pool_scale) @ w_pool_out
    y_conv = conformer_conv(u_conv, conv_dw_w, conv_dw_b, conv_ln_g, conv_ln_b, w_conv_out)
    y_attn = dsa_attention(q, k, v, qi, ki, wi) @ w_attn_out
    g = jax.nn.sigmoid(gates.astype(jnp.float32)).reshape(B, L, N_BRANCH, D_MODEL).astype(h.dtype)
    merged = g[:, :, 0] * y_pool + g[:, :, 1] * y_conv + g[:, :, 2] * y_attn
    return merged @ w_o


def conv_ffn(h, w_up, ffn_dw_w, ffn_dw_b, w_down):
    u = causal_dwconv(h @ w_up, ffn_dw_w, ffn_dw_b)
    gate, val = jnp.split(u, 2, axis=-1)
    return (jax.nn.silu(gate) * val) @ w_down


def setup_inputs(seed: int = 0) -> dict:
    key = jax.random.key(seed)
    ks = jax.random.split(key, 24)
    f32 = jnp.float32

    def nrm(k, shape, scale):
        return jax.random.normal(k, shape, f32) * scale

    return {
        "x": nrm(ks[0], (BATCH, SEQ, D_MODEL), 1.0),
        "positions": jnp.broadcast_to(jnp.arange(SEQ, dtype=jnp.int32), (BATCH, SEQ)),
        "ln_in_g": 1.0 + nrm(ks[1], (D_MODEL,), 0.02),
        "ln_in_b": nrm(ks[2], (D_MODEL,), 0.02),
        "w_in": nrm(ks[3], (DEPTH, D_MODEL, D_IN), D_MODEL ** -0.5),
        "b_in": nrm(ks[4], (DEPTH, D_IN), 0.02),
        "pool_w": nrm(ks[5], (DEPTH, POOL_GROUPS, POOL_GROUP_DIM, POOL_GROUP_DIM), POOL_GROUP_DIM ** -0.5),
        "pool_scale": 1.0 + nrm(ks[6], (DEPTH, D_POOL), 0.1),
        "w_pool_out": nrm(ks[7], (DEPTH, D_POOL, D_MODEL), D_POOL ** -0.5),
        "conv_dw_w": nrm(ks[8], (DEPTH, CONV_WIDTH, D_CONV), CONV_WIDTH ** -0.5),
        "conv_dw_b": nrm(ks[9], (DEPTH, D_CONV), 0.02),
        "conv_ln_g": 1.0 + nrm(ks[10], (DEPTH, D_CONV), 0.02),
        "conv_ln_b": nrm(ks[11], (DEPTH, D_CONV), 0.02),
        "w_conv_out": nrm(ks[12], (DEPTH, D_CONV, D_MODEL), D_CONV ** -0.5),
        "w_attn_out": nrm(ks[13], (DEPTH, D_ATTN, D_MODEL), D_ATTN ** -0.5),
        "w_o": nrm(ks[14], (DEPTH, D_MODEL, D_MODEL), DEEPNORM_BETA * D_MODEL ** -0.5),
        "ln1_g": 1.0 + nrm(ks[15], (DEPTH, D_MODEL), 0.02),
        "ln1_b": nrm(ks[16], (DEPTH, D_MODEL), 0.02),
        "w_up": nrm(ks[17], (DEPTH, D_MODEL, 2 * D_FF), D_MODEL ** -0.5),
        "ffn_dw_w": nrm(ks[18], (DEPTH, FFN_CONV_WIDTH, 2 * D_FF), FFN_CONV_WIDTH ** -0.5),
        "ffn_dw_b": nrm(ks[19], (DEPTH, 2 * D_FF), 0.02),
        "w_down": nrm(ks[20], (DEPTH, D_FF, D_MODEL), DEEPNORM_BETA * D_FF ** -0.5),
        "ln2_g": 1.0 + nrm(ks[21], (DEPTH, D_MODEL), 0.02),
        "ln2_b": nrm(ks[22], (DEPTH, D_MODEL), 0.02),
    }


def reference(x, positions, ln_in_g, ln_in_b, w_in, b_in, pool_w, pool_scale, w_pool_out,
              conv_dw_w, conv_dw_b, conv_ln_g, conv_ln_b, w_conv_out, w_attn_out, w_o,
              ln1_g, ln1_b, w_up, ffn_dw_w, ffn_dw_b, w_down, ln2_g, ln2_b):
    h = layer_norm(x, ln_in_g, ln_in_b)
    for l in range(DEPTH):
        mix = token_mix(h, positions, w_in[l], b_in[l], pool_w[l], pool_scale[l], w_pool_out[l],
                        conv_dw_w[l], conv_dw_b[l], conv_ln_g[l], conv_ln_b[l], w_conv_out[l],
                        w_attn_out[l], w_o[l])
        h = layer_norm(DEEPNORM_ALPHA * h + mix, ln1_g[l], ln1_b[l])
        ffn = conv_ffn(h, w_up[l], ffn_dw_w[l], ffn_dw_b[l], w_down[l])
        h = layer_norm(DEEPNORM_ALPHA * h + ffn, ln2_g[l], ln2_b[l])
    return h
```

```python
import functools

import jax
import jax.numpy as jnp
import numpy as np
from jax import lax
from jax.experimental import pallas as pl
from jax.experimental.pallas import tpu as pltpu

F32 = jnp.float32
BF16 = jnp.bfloat16

D_MODEL = 1024
POOL_GROUPS = 4
POOL_GROUP_DIM = 64
POOL_WINDOWS = (2, 4, 8, 16)
D_POOL = POOL_GROUPS * POOL_GROUP_DIM
D_CONV = 256
CONV_WIDTH = 31
N_HEADS = 8
N_KV_HEADS = 2
HEAD_DIM = 64
D_ATTN = N_HEADS * HEAD_DIM
D_KV = N_KV_HEADS * HEAD_DIM
IDX_HEADS = 8
IDX_DIM = 64
TOPK_MAX = 256
ROPE_THETA = 10000.0
N_BRANCH = 3
D_FF = 2816
FFN_CONV_WIDTH = 3
LN_EPS = 1e-5

LANES = 128
SUBLANES = 8
VMEM_LIMIT_BYTES = 56 * 1024 * 1024

TM = 512
TQ = 128
KC = 512
FC = 256
POOL_HALO = 16
CONV_HALO = 32
FFN_HALO = 8

SEG_POOL = (0, 256)
SEG_CONV = (256, 768)
SEG_Q = (768, 1280)
SEG_KV = (1280, 1536)
SEG_QI = (1536, 2048)
SEG_KI = (2048, 2176)
SEG_WI = (2176, 2304)
D_A = 2304

INT_MIN = -(2 ** 31)
NEG_BIG = -1e30


def _layer_norm(x, g, b):
    mu = jnp.mean(x, axis=-1, keepdims=True)
    xc = x - mu
    var = jnp.mean(xc * xc, axis=-1, keepdims=True)
    return xc * lax.rsqrt(var + LN_EPS) * g + b


def _compiler_params(n_axes):
    return pltpu.CompilerParams(
        dimension_semantics=("arbitrary",) * n_axes,
        vmem_limit_bytes=VMEM_LIMIT_BYTES,
    )


def _resident():
    return pl.BlockSpec(memory_space=pltpu.VMEM)


def _ln_kernel(x_ref, g_ref, b_ref, o_ref):
    o_ref[...] = _layer_norm(x_ref[...], g_ref[...], b_ref[...])


def _ln_call(x2d, g, b):
    n = x2d.shape[0]
    return pl.pallas_call(
        _ln_kernel,
        grid=(n // TM,),
        in_specs=[pl.BlockSpec((TM, D_MODEL), lambda i: (i, 0)), _resident(), _resident()],
        out_specs=pl.BlockSpec((TM, D_MODEL), lambda i: (i, 0)),
        out_shape=jax.ShapeDtypeStruct((n, D_MODEL), F32),
        compiler_params=_compiler_params(1),
        name="ln_in",
    )(x2d, g.reshape(1, -1), b.reshape(1, -1))


def _rope_table_kernel(pos_ref, freq_ref, sign_ref, cos_ref, sin_ref):
    ang = pos_ref[...].astype(F32) * freq_ref[...]
    cos_ref[...] = jnp.cos(ang)
    sin_ref[...] = jnp.sin(ang) * sign_ref[...]


def _rope_tables(positions):
    n = positions.size
    half = HEAD_DIM // 2
    inv_freq = ROPE_THETA ** (-jnp.arange(half, dtype=F32) / half)
    freq_row = jnp.tile(inv_freq, LANES // half).reshape(1, LANES)
    lane = np.arange(LANES)
    sign_row = jnp.asarray(np.where(lane % HEAD_DIM < half, -1.0, 1.0), F32).reshape(1, LANES)
    rows = 1024
    return pl.pallas_call(
        _rope_table_kernel,
        grid=(n // rows,),
        in_specs=[pl.BlockSpec((rows, 1), lambda i: (i, 0)), _resident(), _resident()],
        out_specs=[pl.BlockSpec((rows, LANES), lambda i: (i, 0))] * 2,
        out_shape=[jax.ShapeDtypeStruct((n, LANES), F32)] * 2,
        compiler_params=_compiler_params(1),
        name="rope_tables",
    )(positions.reshape(n, 1), freq_row, sign_row)


def _rope128(x, cos, sin_signed, lower_half):
    partner = jnp.where(lower_half, pltpu.roll(x, LANES - HEAD_DIM // 2, 1), pltpu.roll(x, HEAD_DIM // 2, 1))
    return x * cos + partner * sin_signed


def _in_kernel(tiles_per_seq, h_ref, cos_ref, sin_ref, wa_ref, ba_ref, poolw_ref, pscale_ref,
               dww_ref, dwb_ref, clng_ref, clnb_ref,
               q_ref, kv_ref, qi_ref, ki_ref, wi_ref, ypre_ref, cpre_ref,
               pool_scr, conv_scr):
    i = pl.program_id(0)
    seq_tile = lax.rem(i, tiles_per_seq)
    first = seq_tile == 0
    hb = h_ref[...].astype(BF16)
    cos = cos_ref[...]
    sin = sin_ref[...]
    lane = lax.broadcasted_iota(jnp.int32, (TM, LANES), 1)
    lower_half = (lane & (HEAD_DIM - 1)) < (HEAD_DIM // 2)

    @pl.when(first)
    def _():
        pool_scr[...] = jnp.zeros_like(pool_scr)
        conv_scr[TM:TM + CONV_HALO, :] = jnp.zeros((CONV_HALO, D_CONV), F32)

    def proj(seg):
        lo, hi = seg
        return jnp.dot(hb, wa_ref[:, lo:hi], preferred_element_type=F32) + ba_ref[:, lo:hi]

    zq = proj(SEG_Q)
    for c in range(D_ATTN // LANES):
        blk = zq[:, c * LANES:(c + 1) * LANES]
        q_ref[:, c * LANES:(c + 1) * LANES] = _rope128(blk, cos, sin, lower_half).astype(BF16)
    zkv = proj(SEG_KV)
    kv_ref[:, 0:LANES] = _rope128(zkv[:, 0:LANES], cos, sin, lower_half).astype(BF16)
    kv_ref[:, LANES:2 * LANES] = zkv[:, LANES:2 * LANES].astype(BF16)
    zqi = proj(SEG_QI)
    for c in range(IDX_HEADS * IDX_DIM // LANES):
        blk = zqi[:, c * LANES:(c + 1) * LANES]
        qi_ref[:, c * LANES:(c + 1) * LANES] = (
            _rope128(blk, cos, sin, lower_half) * (IDX_DIM ** -0.5)).astype(BF16)
    ki_ref[...] = _rope128(proj(SEG_KI), cos, sin, lower_half).astype(BF16)
    wi_ref[...] = proj(SEG_WI) * (IDX_HEADS ** -0.5)

    u = proj(SEG_POOL)
    ext = jnp.concatenate([pool_scr[...], u], axis=0)
    pool_scr[...] = u[TM - POOL_HALO:, :]
    s2 = ext + pltpu.roll(ext, 1, 0)
    s4 = s2 + pltpu.roll(s2, 2, 0)
    s8 = s4 + pltpu.roll(s4, 4, 0)
    s16 = s8 + pltpu.roll(s8, 8, 0)
    row = lax.broadcasted_iota(jnp.int32, (TM, D_POOL), 0) + seq_tile * TM
    lane_p = lax.broadcasted_iota(jnp.int32, (TM, D_POOL), 1)
    grp = lane_p >> 6
    win = jnp.where(grp == 0, POOL_WINDOWS[0],
                    jnp.where(grp == 1, POOL_WINDOWS[1],
                              jnp.where(grp == 2, POOL_WINDOWS[2], POOL_WINDOWS[3])))
    cnt = jnp.minimum(row + 1, win).astype(F32)
    wsum = jnp.where(grp == 0, s2[POOL_HALO:], jnp.where(grp == 1, s4[POOL_HALO:],
                     jnp.where(grp == 2, s8[POOL_HALO:], s16[POOL_HALO:])))
    mixed = wsum / cnt - u
    y = jnp.dot(mixed.astype(BF16), poolw_ref[...], preferred_element_type=F32)
    ypre_ref[...] = (y * pscale_ref[...]).astype(BF16)

    uc = proj(SEG_CONV)
    glu = uc[:, :D_CONV] * jax.nn.sigmoid(uc[:, D_CONV:])
    conv_scr[0:CONV_HALO, :] = conv_scr[TM:TM + CONV_HALO, :]
    conv_scr[CONV_HALO:CONV_HALO + TM, :] = glu
    acc = jnp.zeros((TM, D_CONV), F32) + dwb_ref[...]
    for j in range(CONV_WIDTH):
        start = CONV_HALO - (CONV_WIDTH - 1) + j
        acc = acc + dww_ref[j:j + 1, :] * conv_scr[start:start + TM, :]
    hc = _layer_norm(acc, clng_ref[...], clnb_ref[...])
    cpre_ref[...] = (hc * jax.nn.sigmoid(hc)).astype(BF16)


def _in_call(h, cos_t, sin_t, wa, ba, poolw, pscale, dww, dwb, clng, clnb, seq_len):
    n = h.shape[0]
    tiles_per_seq = seq_len // TM
    row_spec = lambda w: pl.BlockSpec((TM, w), lambda i: (i, 0))
    out_widths = (D_ATTN, 2 * LANES, IDX_HEADS * IDX_DIM, LANES, LANES, D_POOL, D_CONV)
    out_dtypes = (BF16, BF16, BF16, BF16, F32, BF16, BF16)
    return pl.pallas_call(
        functools.partial(_in_kernel, tiles_per_seq),
        grid=(n // TM,),
        in_specs=[row_spec(D_MODEL), row_spec(LANES), row_spec(LANES)] + [_resident()] * 8,
        out_specs=[row_spec(w) for w in out_widths],
        out_shape=[jax.ShapeDtypeStruct((n, w), d) for w, d in zip(out_widths, out_dtypes)],
        scratch_shapes=[pltpu.VMEM((POOL_HALO, D_POOL), F32),
                        pltpu.VMEM((TM + CONV_HALO, D_CONV), F32)],
        compiler_params=_compiler_params(1),
        name="in_proj",
    )(h, cos_t, sin_t, wa, ba, poolw, pscale, dww, dwb, clng, clnb)


def _attn_kernel(topk, q_ref, qi_ref, wi_ref, kv_ref, ki_ref, o_ref, key_scr, bias_scr):
    i = pl.program_id(1)
    n_chunks = (i * TQ) // KC + 1
    lane = lax.broadcasted_iota(jnp.int32, (TQ, LANES), 1)
    lower = lane < HEAD_DIM
    qpos = i * TQ + lax.broadcasted_iota(jnp.int32, (TQ, KC), 0)
    kidx = lax.broadcasted_iota(jnp.int32, (TQ, KC), 1)
    nt_dims = (((1,), (1,)), ((), ()))

    qi = qi_ref[...]
    wi = wi_ref[...]
    zero_b = jnp.zeros((TQ, LANES), BF16)
    qi_heads = []
    for h in range(IDX_HEADS):
        blk = qi[:, (h // 2) * LANES:(h // 2 + 1) * LANES]
        qi_heads.append(jnp.where(lower if h % 2 == 0 else jnp.logical_not(lower), blk, zero_b))

    def score_chunk(kc, carry):
        k0 = pl.multiple_of(kc * KC, KC)
        kic = ki_ref[pl.ds(k0, KC), :]
        acc = jnp.zeros((TQ, KC), F32)
        for h in range(IDX_HEADS):
            logits = lax.dot_general(qi_heads[h], kic, nt_dims, preferred_element_type=F32)
            acc = acc + jnp.maximum(logits, 0.0) * wi[:, h:h + 1]
        bits = lax.bitcast_convert_type(acc, jnp.int32)
        key = bits ^ ((bits >> 31) & 0x7FFFFFFF)
        key_scr[kc] = jnp.where(kidx + k0 <= qpos, key, INT_MIN)
        return carry

    lax.fori_loop(0, n_chunks, score_chunk, 0)

    def count(pred):
        def body(kc, cnt):
            k0 = kc * KC
            hit = pred(key_scr[kc], kidx + k0)
            return cnt + jnp.sum(jnp.where(hit, 1.0, 0.0), axis=-1, keepdims=True)
        return lax.fori_loop(0, n_chunks, body, jnp.zeros((TQ, 1), F32))

    kf = float(topk)
    thr0 = jnp.where(count(lambda key, kp: key >= 0) >= kf, 0, INT_MIN).astype(jnp.int32)

    def thr_step(it, thr):
        cand = thr + (jnp.int32(1) << (30 - it))
        c = count(lambda key, kp: key >= cand)
        return jnp.where(c >= kf, cand, thr)

    thr = lax.fori_loop(0, 31, thr_step, thr0)

    need = kf - count(lambda key, kp: key > thr)

    def tie_step(it, jmax):
        cand = jmax + (jnp.int32(1) << (10 - it))
        c = count(lambda key, kp: (key == thr) & (kp < cand))
        return jnp.where(c < need, cand, jmax)

    jmax = lax.fori_loop(0, 11, tie_step, jnp.zeros((TQ, 1), jnp.int32))

    def bias_chunk(kc, carry):
        k0 = kc * KC
        key = key_scr[kc]
        kp = kidx + k0
        sel = ((key > thr) | ((key == thr) & (kp <= jmax))) & (kp <= qpos)
        bias_scr[kc] = jnp.where(sel, 0.0, NEG_BIG)
        return carry

    lax.fori_loop(0, n_chunks, bias_chunk, 0)

    qv = q_ref[...]
    rows = (N_HEADS // N_KV_HEADS) * TQ
    qs = []
    for n in range(N_KV_HEADS):
        keep = lower if n == 0 else jnp.logical_not(lower)
        parts = [jnp.where(keep, qv[:, c * LANES:(c + 1) * LANES], zero_b) for c in range(D_ATTN // LANES)]
        qs.append(jnp.concatenate(parts, axis=0) * (HEAD_DIM ** -0.5))

    def attn_chunk(kc, carry):
        k0 = pl.multiple_of(kc * KC, KC)
        kvc = kv_ref[pl.ds(k0, KC), :]
        kch = kvc[:, 0:LANES]
        vch = kvc[:, LANES:2 * LANES]
        bias = bias_scr[kc]
        bias4 = jnp.concatenate([bias] * (N_HEADS // N_KV_HEADS), axis=0)
        out = []
        for n in range(N_KV_HEADS):
            m, l, acc = carry[n]
            s = lax.dot_general(qs[n], kch, nt_dims, preferred_element_type=F32) + bias4
            m_new = jnp.maximum(m, jnp.max(s, axis=-1, keepdims=True))
            p = jnp.exp(s - m_new)
            alpha = jnp.exp(m - m_new)
            l_new = alpha * l + jnp.sum(p, axis=-1, keepdims=True)
            acc_new = alpha * acc + jnp.dot(p.astype(BF16), vch, preferred_element_type=F32)
            out.append((m_new, l_new, acc_new))
        return tuple(out)

    init = tuple((jnp.full((rows, 1), NEG_BIG, F32), jnp.zeros((rows, 1), F32),
                  jnp.zeros((rows, LANES), F32)) for _ in range(N_KV_HEADS))
    res = lax.fori_loop(0, n_chunks, attn_chunk, init)
    o0 = res[0][2] / res[0][1]
    o1 = res[1][2] / res[1][1]
    for c in range(D_ATTN // LANES):
        blk = jnp.where(lower, o0[c * TQ:(c + 1) * TQ], o1[c * TQ:(c + 1) * TQ])
        o_ref[:, c * LANES:(c + 1) * LANES] = blk.astype(BF16)


def _attn_call(q, qi, wi, kv, ki, batch, seq_len):
    n = q.shape[0]
    nq = seq_len // TQ
    topk = min(TOPK_MAX, seq_len // 4)
    qspec = lambda w: pl.BlockSpec((TQ, w), lambda b, i: (b * nq + i, 0))
    sspec = lambda w: pl.BlockSpec((seq_len, w), lambda b, i: (b, 0))
    return pl.pallas_call(
        functools.partial(_attn_kernel, topk),
        grid=(batch, nq),
        in_specs=[qspec(D_ATTN), qspec(IDX_HEADS * IDX_DIM), qspec(LANES), sspec(2 * LANES), sspec(LANES)],
        out_specs=qspec(D_ATTN),
        out_shape=jax.ShapeDtypeStruct((n, D_ATTN), BF16),
        scratch_shapes=[pltpu.VMEM((seq_len // KC, TQ, KC), jnp.int32),
                        pltpu.VMEM((seq_len // KC, TQ, KC), F32)],
        compiler_params=_compiler_params(2),
        name="dsa_attn",
    )(q, qi, wi, kv, ki)


def _mix_kernel(alpha, h_ref, o_ref, ypre_ref, cpre_ref, wg_ref, bg_ref, wpo_ref, wco_ref, wao_ref,
                wo_ref, g_ref, b_ref, out_ref):
    h = h_ref[...]
    hb = h.astype(BF16)
    branches = ((ypre_ref, wpo_ref), (cpre_ref, wco_ref), (o_ref, wao_ref))
    merged = jnp.zeros((TM, D_MODEL), F32)
    for n, (x_ref, w_ref) in enumerate(branches):
        gate = jnp.dot(hb, wg_ref[:, n * D_MODEL:(n + 1) * D_MODEL], preferred_element_type=F32)
        gate = jax.nn.sigmoid(gate + bg_ref[:, n * D_MODEL:(n + 1) * D_MODEL])
        merged = merged + gate * jnp.dot(x_ref[...], w_ref[...], preferred_element_type=F32)
    mix = jnp.dot(merged.astype(BF16), wo_ref[...], preferred_element_type=F32)
    out_ref[...] = _layer_norm(alpha * h + mix, g_ref[...], b_ref[...])


def _mix_call(alpha, h, o, ypre, cpre, wg, bg, wpo, wco, wao, wo, g, b):
    n = h.shape[0]
    row_spec = lambda w: pl.BlockSpec((TM, w), lambda i: (i, 0))
    return pl.pallas_call(
        functools.partial(_mix_kernel, alpha),
        grid=(n // TM,),
        in_specs=[row_spec(D_MODEL), row_spec(D_ATTN), row_spec(D_POOL), row_spec(D_CONV)] + [_resident()] * 8,
        out_specs=row_spec(D_MODEL),
        out_shape=jax.ShapeDtypeStruct((n, D_MODEL), F32),
        compiler_params=_compiler_params(1),
        name="gated_merge",
    )(h, o, ypre, cpre, wg, bg, wpo, wco, wao, wo, g, b)


def _ffn_kernel(alpha, tiles_per_seq, h_ref, wug_ref, wuv_ref, dwg_ref, dwv_ref, dbg_ref, dbv_ref,
                wd_ref, g_ref, b_ref, out_ref, carry_g, carry_v, acc_ref):
    i = pl.program_id(0)
    first = lax.rem(i, tiles_per_seq) == 0
    h = h_ref[...]
    hb = h.astype(BF16)
    acc_ref[...] = jnp.zeros_like(acc_ref)

    @pl.when(first)
    def _():
        carry_g[...] = jnp.zeros_like(carry_g)
        carry_v[...] = jnp.zeros_like(carry_v)

    def conv3(u, carry_ref, c, w_ref, b_ref2):
        prev = carry_ref[c]
        carry_ref[c] = u[TM - FFN_HALO:, :]
        ext = jnp.concatenate([prev, u], axis=0)
        w = w_ref[c]
        y = (w[2:3, :] * u + w[1:2, :] * pltpu.roll(ext, 1, 0)[FFN_HALO:]
             + w[0:1, :] * pltpu.roll(ext, 2, 0)[FFN_HALO:])
        return y + b_ref2[c]

    def chunk(c, carry):
        ug = jnp.dot(hb, wug_ref[c], preferred_element_type=F32)
        uv = jnp.dot(hb, wuv_ref[c], preferred_element_type=F32)
        gate = conv3(ug, carry_g, c, dwg_ref, dbg_ref)
        val = conv3(uv, carry_v, c, dwv_ref, dbv_ref)
        act = (gate * jax.nn.sigmoid(gate) * val).astype(BF16)
        acc_ref[...] += jnp.dot(act, wd_ref[c], preferred_element_type=F32)
        return carry

    lax.fori_loop(0, D_FF // FC, chunk, 0)
    out_ref[...] = _layer_norm(alpha * h + acc_ref[...], g_ref[...], b_ref[...])


def _ffn_call(alpha, h, wug, wuv, dwg, dwv, dbg, dbv, wd, g, b, seq_len):
    n = h.shape[0]
    nc = D_FF // FC
    row_spec = pl.BlockSpec((TM, D_MODEL), lambda i: (i, 0))
    return pl.pallas_call(
        functools.partial(_ffn_kernel, alpha, seq_len // TM),
        grid=(n // TM,),
        in_specs=[row_spec] + [_resident()] * 9,
        out_specs=row_spec,
        out_shape=jax.ShapeDtypeStruct((n, D_MODEL), F32),
        scratch_shapes=[pltpu.VMEM((nc, FFN_HALO, FC), F32), pltpu.VMEM((nc, FFN_HALO, FC), F32),
                        pltpu.VMEM((TM, D_MODEL), F32)],
        compiler_params=_compiler_params(1),
        name="conv_ffn",
    )(h, wug, wuv, dwg, dwv, dbg, dbv, wd, g, b)


def _head_pair_perm():
    group = N_HEADS // N_KV_HEADS
    cols = []
    for j in range(group):
        cols.extend(range(j * HEAD_DIM, (j + 1) * HEAD_DIM))
        cols.extend(range((group + j) * HEAD_DIM, (group + j + 1) * HEAD_DIM))
    return np.asarray(cols)


def _prep_in_weights(w_in, b_in):
    sizes = (D_POOL, 2 * D_CONV, D_ATTN, D_KV, D_KV, IDX_HEADS * IDX_DIM, IDX_DIM, IDX_HEADS, N_BRANCH * D_MODEL)
    offs = np.concatenate([[0], np.cumsum(sizes)])
    seg = lambda a, k: a[..., offs[k]:offs[k + 1]]
    perm = _head_pair_perm()

    def relayout(a):
        pad = jnp.zeros(a.shape[:-1] + (LANES - IDX_HEADS,), a.dtype)
        return jnp.concatenate([seg(a, 0), seg(a, 1), seg(a, 2)[..., perm], seg(a, 3), seg(a, 4), seg(a, 5),
                                seg(a, 6), seg(a, 6), seg(a, 7), pad], axis=-1)

    wa = relayout(w_in).astype(BF16)
    ba = relayout(b_in[None, :])
    wg = seg(w_in, 8).astype(BF16)
    bg = seg(b_in[None, :], 8)
    return wa, ba, wg, bg


def _block_diag(pool_w):
    out = jnp.zeros((D_POOL, D_POOL), pool_w.dtype)
    for g in range(POOL_GROUPS):
        lo = g * POOL_GROUP_DIM
        out = out.at[lo:lo + POOL_GROUP_DIM, lo:lo + POOL_GROUP_DIM].set(pool_w[g])
    return out


def kernel(x, positions, ln_in_g, ln_in_b, w_in, b_in, pool_w, pool_scale, w_pool_out, conv_dw_w, conv_dw_b,
           conv_ln_g, conv_ln_b, w_conv_out, w_attn_out, w_o, ln1_g, ln1_b, w_up, ffn_dw_w, ffn_dw_b, w_down,
           ln2_g, ln2_b):
    batch, seq_len, d_model = x.shape
    depth = w_in.shape[0]
    assert d_model == D_MODEL and seq_len % TM == 0 and seq_len % KC == 0 and KC % TQ == 0
    n = batch * seq_len
    alpha = float((2 * depth) ** 0.25)
    nc = D_FF // FC
    perm = _head_pair_perm()
    row = lambda a: a.reshape(1, -1)

    cos_t, sin_t = _rope_tables(positions)
    h = _ln_call(x.reshape(n, D_MODEL), ln_in_g, ln_in_b)
    for l in range(depth):
        wa, ba, wg, bg = _prep_in_weights(w_in[l], b_in[l])
        q, kv, qi, ki, wi, ypre, cpre = _in_call(
            h, cos_t, sin_t, wa, ba, _block_diag(pool_w[l]).astype(BF16), row(pool_scale[l]),
            conv_dw_w[l], row(conv_dw_b[l]), row(conv_ln_g[l]), row(conv_ln_b[l]), seq_len)
        o = _attn_call(q, qi, wi, kv, ki, batch, seq_len)
        h = _mix_call(alpha, h, o, ypre, cpre, wg, bg, w_pool_out[l].astype(BF16), w_conv_out[l].astype(BF16),
                      w_attn_out[l][perm].astype(BF16), w_o[l].astype(BF16), row(ln1_g[l]), row(ln1_b[l]))
        split = lambda a, lo: a[..., lo:lo + D_FF].reshape(a.shape[:-1] + (nc, FC))
        wug = jnp.moveaxis(split(w_up[l], 0), 1, 0).astype(BF16)
        wuv = jnp.moveaxis(split(w_up[l], D_FF), 1, 0).astype(BF16)
        dwg = jnp.moveaxis(split(ffn_dw_w[l], 0), 1, 0)
        dwv = jnp.moveaxis(split(ffn_dw_w[l], D_FF), 1, 0)
        dbg = split(ffn_dw_b[l], 0).reshape(nc, 1, FC)
        dbv = split(ffn_dw_b[l], D_FF).reshape(nc, 1, FC)
        wd = w_down[l].reshape(nc, FC, D_MODEL).astype(BF16)
        h = _ffn_call(alpha, h, wug, wuv, dwg, dwv, dbg, dbv, wd, row(ln2_g[l]), row(ln2_b[l]), seq_len)
    return h.reshape(batch, seq_len, D_MODEL)
```

```python
import functools

import jax
import jax.numpy as jnp
import numpy as np
from jax import lax
from jax.experimental import pallas as pl
from jax.experimental.pallas import tpu as pltpu

F32 = jnp.float32
BF16 = jnp.bfloat16

D_MODEL = 1024
POOL_GROUPS = 4
POOL_GROUP_DIM = 64
POOL_WINDOWS = (2, 4, 8, 16)
D_POOL = POOL_GROUPS * POOL_GROUP_DIM
D_CONV = 256
CONV_WIDTH = 31
N_HEADS = 8
N_KV_HEADS = 2
HEAD_DIM = 64
D_ATTN = N_HEADS * HEAD_DIM
D_KV = N_KV_HEADS * HEAD_DIM
IDX_HEADS = 8
IDX_DIM = 64
TOPK_MAX = 256
ROPE_THETA = 10000.0
N_BRANCH = 3
D_FF = 2816
FFN_CONV_WIDTH = 3
LN_EPS = 1e-5

LANES = 128
SUBLANES = 8
VMEM_LIMIT_BYTES = 56 * 1024 * 1024

TM = 512
TQ = 256
COUNT_ROWS = 32
WI_ROWS = 8
KC = 512
FC = 256
POOL_HALO = 16
CONV_HALO = 32
FFN_HALO = 8

SEG_POOL = (0, 256)
SEG_CONV = (256, 768)
SEG_Q = (768, 1280)
SEG_KV = (1280, 1536)
SEG_QI = (1536, 2048)
SEG_KI = (2048, 2176)
SEG_WI = (2176, 2304)
D_A = 2304

INT_MIN = -(2 ** 31)
NEG_BIG = -1e30


def _layer_norm(x, g, b):
    mu = jnp.mean(x, axis=-1, keepdims=True)
    xc = x - mu
    var = jnp.mean(xc * xc, axis=-1, keepdims=True)
    return xc * lax.rsqrt(var + LN_EPS) * g + b


def _compiler_params(n_axes):
    return pltpu.CompilerParams(
        dimension_semantics=("arbitrary",) * n_axes,
        vmem_limit_bytes=VMEM_LIMIT_BYTES,
    )


def _resident():
    return pl.BlockSpec(memory_space=pltpu.VMEM)


def _ln_kernel(x_ref, g_ref, b_ref, o_ref):
    o_ref[...] = _layer_norm(x_ref[...], g_ref[...], b_ref[...])


def _ln_call(x2d, g, b):
    n = x2d.shape[0]
    return pl.pallas_call(
        _ln_kernel,
        grid=(n // TM,),
        in_specs=[pl.BlockSpec((TM, D_MODEL), lambda i: (i, 0)), _resident(), _resident()],
        out_specs=pl.BlockSpec((TM, D_MODEL), lambda i: (i, 0)),
        out_shape=jax.ShapeDtypeStruct((n, D_MODEL), F32),
        compiler_params=_compiler_params(1),
        name="ln_in",
    )(x2d, g.reshape(1, -1), b.reshape(1, -1))


def _rope_table_kernel(pos_ref, freq_ref, sign_ref, cos_ref, sin_ref):
    ang = pos_ref[...].astype(F32) * freq_ref[...]
    cos_ref[...] = jnp.cos(ang)
    sin_ref[...] = jnp.sin(ang) * sign_ref[...]


def _rope_tables(positions):
    n = positions.size
    half = HEAD_DIM // 2
    inv_freq = ROPE_THETA ** (-jnp.arange(half, dtype=F32) / half)
    freq_row = jnp.tile(inv_freq, LANES // half).reshape(1, LANES)
    lane = np.arange(LANES)
    sign_row = jnp.asarray(np.where(lane % HEAD_DIM < half, -1.0, 1.0), F32).reshape(1, LANES)
    rows = 1024
    return pl.pallas_call(
        _rope_table_kernel,
        grid=(n // rows,),
        in_specs=[pl.BlockSpec((rows, 1), lambda i: (i, 0)), _resident(), _resident()],
        out_specs=[pl.BlockSpec((rows, LANES), lambda i: (i, 0))] * 2,
        out_shape=[jax.ShapeDtypeStruct((n, LANES), F32)] * 2,
        compiler_params=_compiler_params(1),
        name="rope_tables",
    )(positions.reshape(n, 1), freq_row, sign_row)


def _rope128(x, cos, sin_signed, lower_half):
    partner = jnp.where(lower_half, pltpu.roll(x, LANES - HEAD_DIM // 2, 1), pltpu.roll(x, HEAD_DIM // 2, 1))
    return x * cos + partner * sin_signed


def _in_kernel(tiles_per_seq, h_ref, cos_ref, sin_ref, wa_ref, ba_ref, poolw_ref, pscale_ref,
               dww_ref, dwb_ref, clng_ref, clnb_ref,
               q_ref, kv_ref, qi_ref, ki_ref, wi_ref, ypre_ref, cpre_ref,
               pool_scr, conv_scr):
    i = pl.program_id(0)
    seq_tile = lax.rem(i, tiles_per_seq)
    first = seq_tile == 0
    hb = h_ref[...].astype(BF16)
    cos = cos_ref[...]
    sin = sin_ref[...]
    lane = lax.broadcasted_iota(jnp.int32, (TM, LANES), 1)
    lower_half = (lane & (HEAD_DIM - 1)) < (HEAD_DIM // 2)

    @pl.when(first)
    def _():
        pool_scr[...] = jnp.zeros_like(pool_scr)
        conv_scr[TM:TM + CONV_HALO, :] = jnp.zeros((CONV_HALO, D_CONV), F32)

    def proj(seg):
        lo, hi = seg
        return jnp.dot(hb, wa_ref[:, lo:hi], preferred_element_type=F32) + ba_ref[:, lo:hi]

    zq = proj(SEG_Q)
    for c in range(D_ATTN // LANES):
        blk = zq[:, c * LANES:(c + 1) * LANES]
        q_ref[:, c * LANES:(c + 1) * LANES] = _rope128(blk, cos, sin, lower_half).astype(BF16)
    zkv = proj(SEG_KV)
    kv_ref[:, 0:LANES] = _rope128(zkv[:, 0:LANES], cos, sin, lower_half).astype(BF16)
    kv_ref[:, LANES:2 * LANES] = zkv[:, LANES:2 * LANES].astype(BF16)
    zqi = proj(SEG_QI)
    for c in range(IDX_HEADS * IDX_DIM // LANES):
        blk = zqi[:, c * LANES:(c + 1) * LANES]
        qi_ref[:, c * LANES:(c + 1) * LANES] = (
            _rope128(blk, cos, sin, lower_half) * (IDX_DIM ** -0.5)).astype(BF16)
    ki_ref[...] = _rope128(proj(SEG_KI), cos, sin, lower_half).astype(BF16)
    wi_t = jnp.transpose(proj(SEG_WI) * (IDX_HEADS ** -0.5))
    wi_ref[...] = wi_t[0:WI_ROWS, :]

    u = proj(SEG_POOL)
    ext = jnp.concatenate([pool_scr[...], u], axis=0)
    pool_scr[...] = u[TM - POOL_HALO:, :]
    s2 = ext + pltpu.roll(ext, 1, 0)
    s4 = s2 + pltpu.roll(s2, 2, 0)
    s8 = s4 + pltpu.roll(s4, 4, 0)
    s16 = s8 + pltpu.roll(s8, 8, 0)
    row = lax.broadcasted_iota(jnp.int32, (TM, D_POOL), 0) + seq_tile * TM
    lane_p = lax.broadcasted_iota(jnp.int32, (TM, D_POOL), 1)
    grp = lane_p >> 6
    win = jnp.where(grp == 0, POOL_WINDOWS[0],
                    jnp.where(grp == 1, POOL_WINDOWS[1],
                              jnp.where(grp == 2, POOL_WINDOWS[2], POOL_WINDOWS[3])))
    cnt = jnp.minimum(row + 1, win).astype(F32)
    wsum = jnp.where(grp == 0, s2[POOL_HALO:], jnp.where(grp == 1, s4[POOL_HALO:],
                     jnp.where(grp == 2, s8[POOL_HALO:], s16[POOL_HALO:])))
    mixed = wsum / cnt - u
    y = jnp.dot(mixed.astype(BF16), poolw_ref[...], preferred_element_type=F32)
    ypre_ref[...] = (y * pscale_ref[...]).astype(BF16)

    uc = proj(SEG_CONV)
    glu = uc[:, :D_CONV] * jax.nn.sigmoid(uc[:, D_CONV:])
    conv_scr[0:CONV_HALO, :] = conv_scr[TM:TM + CONV_HALO, :]
    conv_scr[CONV_HALO:CONV_HALO + TM, :] = glu
    acc = jnp.zeros((TM, D_CONV), F32) + dwb_ref[...]
    for j in range(CONV_WIDTH):
        start = CONV_HALO - (CONV_WIDTH - 1) + j
        acc = acc + dww_ref[j:j + 1, :] * conv_scr[start:start + TM, :]
    hc = _layer_norm(acc, clng_ref[...], clnb_ref[...])
    cpre_ref[...] = (hc * jax.nn.sigmoid(hc)).astype(BF16)


def _in_call(h, cos_t, sin_t, wa, ba, poolw, pscale, dww, dwb, clng, clnb, seq_len):
    n = h.shape[0]
    tiles_per_seq = seq_len // TM
    row_spec = lambda w: pl.BlockSpec((TM, w), lambda i: (i, 0))
    out_widths = (D_ATTN, 2 * LANES, IDX_HEADS * IDX_DIM, LANES, D_POOL, D_CONV)
    out_dtypes = (BF16, BF16, BF16, BF16, BF16, BF16)
    out_specs = [row_spec(w) for w in out_widths]
    out_shape = [jax.ShapeDtypeStruct((n, w), d) for w, d in zip(out_widths, out_dtypes)]
    out_specs.insert(4, pl.BlockSpec((WI_ROWS, TM), lambda i: (0, i)))
    out_shape.insert(4, jax.ShapeDtypeStruct((WI_ROWS, n), F32))
    return pl.pallas_call(
        functools.partial(_in_kernel, tiles_per_seq),
        grid=(n // TM,),
        in_specs=[row_spec(D_MODEL), row_spec(LANES), row_spec(LANES)] + [_resident()] * 8,
        out_specs=out_specs,
        out_shape=out_shape,
        scratch_shapes=[pltpu.VMEM((POOL_HALO, D_POOL), F32),
                        pltpu.VMEM((TM + CONV_HALO, D_CONV), F32)],
        compiler_params=_compiler_params(1),
        name="in_proj",
    )(h, cos_t, sin_t, wa, ba, poolw, pscale, dww, dwb, clng, clnb)


def _attn_kernel(topk, seq_len, q_ref, qi_ref, wit_ref, kv_ref, ki_ref, o_ref,
                 key_scr, bias_scr, qi_scr, jmax_scr):
    i = pl.program_id(1)
    n_chunks = (i * TQ + TQ - 1) // KC + 1
    lane = lax.broadcasted_iota(jnp.int32, (TQ, LANES), 1)
    lower = lane < HEAD_DIM
    nt_dims = (((1,), (1,)), ((), ()))
    zero_b = jnp.zeros((TQ, LANES), BF16)
    kpos_t = lax.broadcasted_iota(jnp.int32, (KC, TQ), 0)
    qpos_t = i * TQ + lax.broadcasted_iota(jnp.int32, (KC, TQ), 1)
    qpos_row = i * TQ + lax.broadcasted_iota(jnp.int32, (1, TQ), 1)

    qi = qi_ref[...]
    for h in range(IDX_HEADS):
        blk = qi[:, (h // 2) * LANES:(h // 2 + 1) * LANES]
        qi_scr[h] = jnp.where(lower if h % 2 == 0 else jnp.logical_not(lower), blk, zero_b)
    wit = wit_ref[...]

    def score_chunk(kc, carry):
        k0 = pl.multiple_of(kc * KC, KC)
        kic = ki_ref[pl.ds(k0, KC), :]
        acc = jnp.zeros((KC, TQ), F32)
        for h in range(IDX_HEADS):
            logits = lax.dot_general(kic, qi_scr[h], nt_dims, preferred_element_type=F32)
            acc = acc + jnp.maximum(logits, 0.0) * wit[h:h + 1, :]
        bits = lax.bitcast_convert_type(acc, jnp.int32)
        key = bits ^ ((bits >> 31) & 0x7FFFFFFF)
        key_scr[kc] = jnp.where(kpos_t + k0 <= qpos_t, key, INT_MIN)
        return carry

    lax.fori_loop(0, n_chunks, score_chunk, 0)

    def count(pred):
        def body(kc, acc):
            hit = pred(key_scr[kc], kc)
            ones = jnp.where(hit, 1.0, 0.0)
            return acc + jnp.sum(ones.reshape(KC // COUNT_ROWS, COUNT_ROWS, TQ), axis=0)
        acc = lax.fori_loop(0, n_chunks, body, jnp.zeros((COUNT_ROWS, TQ), F32))
        return jnp.sum(acc, axis=0, keepdims=True)

    kf = float(topk)
    thr0 = jnp.where(count(lambda key, kc: key >= 0) >= kf, 0, INT_MIN).astype(jnp.int32)

    def thr_step(it, thr):
        cand = thr + (jnp.int32(1) << (30 - it))
        c = count(lambda key, kc: key >= cand)
        return jnp.where(c >= kf, cand, thr)

    thr = lax.fori_loop(0, 31, thr_step, thr0)

    cnt_ge = count(lambda key, kc: key >= thr)
    need = kf - count(lambda key, kc: key > thr)
    jmax_scr[...] = jnp.full((SUBLANES, TQ), seq_len, jnp.int32)

    @pl.when(jnp.max(jnp.where(thr == INT_MIN, 0.0, cnt_ge)) > kf)
    def _():
        tie_bits = (seq_len - 1).bit_length()

        def tie_step(it, jmax):
            cand = jmax + (jnp.int32(1) << (tie_bits - 1 - it))
            c = count(lambda key, kc: (key == thr) & (kpos_t + kc * KC < cand))
            return jnp.where(c < need, cand, jmax)

        jmax = lax.fori_loop(0, tie_bits, tie_step, jnp.zeros((1, TQ), jnp.int32))
        jmax_scr[...] = jnp.broadcast_to(jmax, (SUBLANES, TQ))

    jmax = jnp.where(thr == INT_MIN, qpos_row, jmax_scr[0:1, :])

    def bias_chunk(kc, carry):
        key = key_scr[kc]
        kp = kpos_t + kc * KC
        sel = (key > thr) | ((key == thr) & (kp <= jmax))
        bias_scr[kc] = jnp.transpose(jnp.where(sel, 0.0, NEG_BIG))
        return carry

    lax.fori_loop(0, n_chunks, bias_chunk, 0)

    qv = q_ref[...]
    rows = (N_HEADS // N_KV_HEADS) * TQ
    qs = []
    for n in range(N_KV_HEADS):
        keep = lower if n == 0 else jnp.logical_not(lower)
        parts = [jnp.where(keep, qv[:, c * LANES:(c + 1) * LANES], zero_b) for c in range(D_ATTN // LANES)]
        qs.append(jnp.concatenate(parts, axis=0) * (HEAD_DIM ** -0.5))

    def attn_chunk(kc, carry):
        k0 = pl.multiple_of(kc * KC, KC)
        kvc = kv_ref[pl.ds(k0, KC), :]
        kch = kvc[:, 0:LANES]
        vch = kvc[:, LANES:2 * LANES]
        bias = bias_scr[kc]
        bias4 = jnp.concatenate([bias] * (N_HEADS // N_KV_HEADS), axis=0)
        out = []
        for n in range(N_KV_HEADS):
            m, l, acc = carry[n]
            s = lax.dot_general(qs[n], kch, nt_dims, preferred_element_type=F32) + bias4
            m_new = jnp.maximum(m, jnp.max(s, axis=-1, keepdims=True))
            p = jnp.exp(s - m_new)
            alpha = jnp.exp(m - m_new)
            l_new = alpha * l + jnp.sum(p, axis=-1, keepdims=True)
            acc_new = alpha * acc + jnp.dot(p.astype(BF16), vch, preferred_element_type=F32)
            out.append((m_new, l_new, acc_new))
        return tuple(out)

    init = tuple((jnp.full((rows, 1), NEG_BIG, F32), jnp.zeros((rows, 1), F32),
                  jnp.zeros((rows, LANES), F32)) for _ in range(N_KV_HEADS))
    res = lax.fori_loop(0, n_chunks, attn_chunk, init)
    o0 = res[0][2] / res[0][1]
    o1 = res[1][2] / res[1][1]
    for c in range(D_ATTN // LANES):
        blk = jnp.where(lower, o0[c * TQ:(c + 1) * TQ], o1[c * TQ:(c + 1) * TQ])
        o_ref[:, c * LANES:(c + 1) * LANES] = blk.astype(BF16)


def _attn_call(q, qi, wi, kv, ki, batch, seq_len):
    n = q.shape[0]
    nq = seq_len // TQ
    topk = min(TOPK_MAX, seq_len // 4)
    qspec = lambda w: pl.BlockSpec((TQ, w), lambda b, i: (b * nq + i, 0))
    sspec = lambda w: pl.BlockSpec((seq_len, w), lambda b, i: (b, 0))
    wspec = pl.BlockSpec((WI_ROWS, TQ), lambda b, i: (0, b * nq + i))
    return pl.pallas_call(
        functools.partial(_attn_kernel, topk, seq_len),
        grid=(batch, nq),
        in_specs=[qspec(D_ATTN), qspec(IDX_HEADS * IDX_DIM), wspec, sspec(2 * LANES), sspec(LANES)],
        out_specs=qspec(D_ATTN),
        out_shape=jax.ShapeDtypeStruct((n, D_ATTN), BF16),
        scratch_shapes=[pltpu.VMEM((seq_len // KC, KC, TQ), jnp.int32),
                        pltpu.VMEM((seq_len // KC, TQ, KC), F32),
                        pltpu.VMEM((IDX_HEADS, TQ, LANES), BF16),
                        pltpu.VMEM((SUBLANES, TQ), jnp.int32)],
        compiler_params=_compiler_params(2),
        name="dsa_attn",
    )(q, qi, wi, kv, ki)


def _mix_kernel(alpha, h_ref, o_ref, ypre_ref, cpre_ref, wg_ref, bg_ref, wpo_ref, wco_ref, wao_ref,
                wo_ref, g_ref, b_ref, out_ref):
    h = h_ref[...]
    hb = h.astype(BF16)
    branches = ((ypre_ref, wpo_ref), (cpre_ref, wco_ref), (o_ref, wao_ref))
    merged = jnp.zeros((TM, D_MODEL), F32)
    for n, (x_ref, w_ref) in enumerate(branches):
        gate = jnp.dot(hb, wg_ref[:, n * D_MODEL:(n + 1) * D_MODEL], preferred_element_type=F32)
        gate = jax.nn.sigmoid(gate + bg_ref[:, n * D_MODEL:(n + 1) * D_MODEL])
        merged = merged + gate * jnp.dot(x_ref[...], w_ref[...], preferred_element_type=F32)
    mix = jnp.dot(merged.astype(BF16), wo_ref[...], preferred_element_type=F32)
    out_ref[...] = _layer_norm(alpha * h + mix, g_ref[...], b_ref[...])


def _mix_call(alpha, h, o, ypre, cpre, wg, bg, wpo, wco, wao, wo, g, b):
    n = h.shape[0]
    row_spec = lambda w: pl.BlockSpec((TM, w), lambda i: (i, 0))
    return pl.pallas_call(
        functools.partial(_mix_kernel, alpha),
        grid=(n // TM,),
        in_specs=[row_spec(D_MODEL), row_spec(D_ATTN), row_spec(D_POOL), row_spec(D_CONV)] + [_resident()] * 8,
        out_specs=row_spec(D_MODEL),
        out_shape=jax.ShapeDtypeStruct((n, D_MODEL), F32),
        compiler_params=_compiler_params(1),
        name="gated_merge",
    )(h, o, ypre, cpre, wg, bg, wpo, wco, wao, wo, g, b)


def _ffn_kernel(alpha, tiles_per_seq, h_ref, wug_ref, wuv_ref, dwg_ref, dwv_ref, dbg_ref, dbv_ref,
                wd_ref, g_ref, b_ref, out_ref, carry_g, carry_v, acc_ref):
    i = pl.program_id(0)
    first = lax.rem(i, tiles_per_seq) == 0
    h = h_ref[...]
    hb = h.astype(BF16)
    acc_ref[...] = jnp.zeros_like(acc_ref)

    @pl.when(first)
    def _():
        carry_g[...] = jnp.zeros_like(carry_g)
        carry_v[...] = jnp.zeros_like(carry_v)

    def conv3(u, carry_ref, c, w_ref, b_ref2):
        prev = carry_ref[c]
        carry_ref[c] = u[TM - FFN_HALO:, :]
        ext = jnp.concatenate([prev, u], axis=0)
        w = w_ref[c]
        y = (w[2:3, :] * u + w[1:2, :] * pltpu.roll(ext, 1, 0)[FFN_HALO:]
             + w[0:1, :] * pltpu.roll(ext, 2, 0)[FFN_HALO:])
        return y + b_ref2[c]

    def chunk(c, carry):
        ug = jnp.dot(hb, wug_ref[c], preferred_element_type=F32)
        uv = jnp.dot(hb, wuv_ref[c], preferred_element_type=F32)
        gate = conv3(ug, carry_g, c, dwg_ref, dbg_ref)
        val = conv3(uv, carry_v, c, dwv_ref, dbv_ref)
        act = (gate * jax.nn.sigmoid(gate) * val).astype(BF16)
        acc_ref[...] += jnp.dot(act, wd_ref[c], preferred_element_type=F32)
        return carry

    lax.fori_loop(0, D_FF // FC, chunk, 0)
    out_ref[...] = _layer_norm(alpha * h + acc_ref[...], g_ref[...], b_ref[...])


def _ffn_call(alpha, h, wug, wuv, dwg, dwv, dbg, dbv, wd, g, b, seq_len):
    n = h.shape[0]
    nc = D_FF // FC
    row_spec = pl.BlockSpec((TM, D_MODEL), lambda i: (i, 0))
    return pl.pallas_call(
        functools.partial(_ffn_kernel, alpha, seq_len // TM),
        grid=(n // TM,),
        in_specs=[row_spec] + [_resident()] * 9,
        out_specs=row_spec,
        out_shape=jax.ShapeDtypeStruct((n, D_MODEL), F32),
        scratch_shapes=[pltpu.VMEM((nc, FFN_HALO, FC), F32), pltpu.VMEM((nc, FFN_HALO, FC), F32),
                        pltpu.VMEM((TM, D_MODEL), F32)],
        compiler_params=_compiler_params(1),
        name="conv_ffn",
    )(h, wug, wuv, dwg, dwv, dbg, dbv, wd, g, b)


def _head_pair_perm():
    group = N_HEADS // N_KV_HEADS
    cols = []
    for j in range(group):
        cols.extend(range(j * HEAD_DIM, (j + 1) * HEAD_DIM))
        cols.extend(range((group + j) * HEAD_DIM, (group + j + 1) * HEAD_DIM))
    return np.asarray(cols)


def _prep_in_weights(w_in, b_in):
    sizes = (D_POOL, 2 * D_CONV, D_ATTN, D_KV, D_KV, IDX_HEADS * IDX_DIM, IDX_DIM, IDX_HEADS, N_BRANCH * D_MODEL)
    offs = np.concatenate([[0], np.cumsum(sizes)])
    seg = lambda a, k: a[..., offs[k]:offs[k + 1]]
    perm = _head_pair_perm()

    def relayout(a):
        pad = jnp.zeros(a.shape[:-1] + (LANES - IDX_HEADS,), a.dtype)
        return jnp.concatenate([seg(a, 0), seg(a, 1), seg(a, 2)[..., perm], seg(a, 3), seg(a, 4), seg(a, 5),
                                seg(a, 6), seg(a, 6), seg(a, 7), pad], axis=-1)

    wa = relayout(w_in).astype(BF16)
    ba = relayout(b_in[None, :])
    wg = seg(w_in, 8).astype(BF16)
    bg = seg(b_in[None, :], 8)
    return wa, ba, wg, bg


def _block_diag(pool_w):
    out = jnp.zeros((D_POOL, D_POOL), pool_w.dtype)
    for g in range(POOL_GROUPS):
        lo = g * POOL_GROUP_DIM
        out = out.at[lo:lo + POOL_GROUP_DIM, lo:lo + POOL_GROUP_DIM].set(pool_w[g])
    return out


def kernel(x, positions, ln_in_g, ln_in_b, w_in, b_in, pool_w, pool_scale, w_pool_out, conv_dw_w, conv_dw_b,
           conv_ln_g, conv_ln_b, w_conv_out, w_attn_out, w_o, ln1_g, ln1_b, w_up, ffn_dw_w, ffn_dw_b, w_down,
           ln2_g, ln2_b):
    batch, seq_len, d_model = x.shape
    depth = w_in.shape[0]
    assert d_model == D_MODEL and seq_len % TM == 0 and seq_len % KC == 0 and KC % TQ == 0
    n = batch * seq_len
    alpha = float((2 * depth) ** 0.25)
    nc = D_FF // FC
    perm = _head_pair_perm()
    row = lambda a: a.reshape(1, -1)

    cos_t, sin_t = _rope_tables(positions)
    h = _ln_call(x.reshape(n, D_MODEL), ln_in_g, ln_in_b)
    for l in range(depth):
        wa, ba, wg, bg = _prep_in_weights(w_in[l], b_in[l])
        q, kv, qi, ki, wi, ypre, cpre = _in_call(
            h, cos_t, sin_t, wa, ba, _block_diag(pool_w[l]).astype(BF16), row(pool_scale[l]),
            conv_dw_w[l], row(conv_dw_b[l]), row(conv_ln_g[l]), row(conv_ln_b[l]), seq_len)
        o = _attn_call(q, qi, wi, kv, ki, batch, seq_len)
        h = _mix_call(alpha, h, o, ypre, cpre, wg, bg, w_pool_out[l].astype(BF16), w_conv_out[l].astype(BF16),
                      w_attn_out[l][perm].astype(BF16), w_o[l].astype(BF16), row(ln1_g[l]), row(ln1_b[l]))
        split = lambda a, lo: a[..., lo:lo + D_FF].reshape(a.shape[:-1] + (nc, FC))
        wug = jnp.moveaxis(split(w_up[l], 0), 1, 0).astype(BF16)
        wuv = jnp.moveaxis(split(w_up[l], D_FF), 1, 0).astype(BF16)
        dwg = jnp.moveaxis(split(ffn_dw_w[l], 0), 1, 0)
        dwv = jnp.moveaxis(split(ffn_dw_w[l], D_FF), 1, 0)
        dbg = split(ffn_dw_b[l], 0).reshape(nc, 1, FC)
        dbv = split(ffn_dw_b[l], D_FF).reshape(nc, 1, FC)
        wd = w_down[l].reshape(nc, FC, D_MODEL).astype(BF16)
        h = _ffn_call(alpha, h, wug, wuv, dwg, dwv, dbg, dbv, wd, row(ln2_g[l]), row(ln2_b[l]), seq_len)
    return h.reshape(batch, seq_len, D_MODEL)
```

```python
import functools

import jax
import jax.numpy as jnp
import numpy as np
from jax import lax
from jax.experimental import pallas as pl
from jax.experimental.pallas import tpu as pltpu

F32 = jnp.float32
BF16 = jnp.bfloat16

D_MODEL = 1024
POOL_GROUPS = 4
POOL_GROUP_DIM = 64
POOL_WINDOWS = (2, 4, 8, 16)
D_POOL = POOL_GROUPS * POOL_GROUP_DIM
D_CONV = 256
CONV_WIDTH = 31
N_HEADS = 8
N_KV_HEADS = 2
HEAD_DIM = 64
D_ATTN = N_HEADS * HEAD_DIM
D_KV = N_KV_HEADS * HEAD_DIM
IDX_HEADS = 8
IDX_DIM = 64
TOPK_MAX = 256
ROPE_THETA = 10000.0
N_BRANCH = 3
D_FF = 2816
FFN_CONV_WIDTH = 3
LN_EPS = 1e-5

LANES = 128
SUBLANES = 8
VMEM_LIMIT_BYTES = 56 * 1024 * 1024

TM = 512
TQ = 256
COUNT_ROWS = 32
WI_ROWS = 8
KC = 512
FC = 256
POOL_HALO = 16
CONV_HALO = 32
FFN_HALO = 8

SEG_POOL = (0, 256)
SEG_CONV = (256, 768)
SEG_Q = (768, 1280)
SEG_KV = (1280, 1536)
SEG_QI = (1536, 2048)
SEG_KI = (2048, 2176)
SEG_WI = (2176, 2304)
D_A = 2304

Q_SCALE = float(HEAD_DIM ** -0.5 * np.log2(np.e))
INT_MIN = -(2 ** 31)
HALF_BIAS = 2 ** 15
NEG_BIG = -1e30


def _layer_norm(x, g, b):
    mu = jnp.mean(x, axis=-1, keepdims=True)
    xc = x - mu
    var = jnp.mean(xc * xc, axis=-1, keepdims=True)
    return xc * lax.rsqrt(var + LN_EPS) * g + b


def _compiler_params(n_axes):
    return pltpu.CompilerParams(
        dimension_semantics=("arbitrary",) * n_axes,
        vmem_limit_bytes=VMEM_LIMIT_BYTES,
    )


def _resident():
    return pl.BlockSpec(memory_space=pltpu.VMEM)


def _rope_table_kernel(pos_ref, freq_ref, sign_ref, cos_ref, sin_ref):
    ang = pos_ref[...].astype(F32) * freq_ref[...]
    cos_ref[...] = jnp.cos(ang)
    sin_ref[...] = jnp.sin(ang) * sign_ref[...]


def _rope_tables(positions):
    n = positions.size
    half = HEAD_DIM // 2
    inv_freq = ROPE_THETA ** (-jnp.arange(half, dtype=F32) / half)
    freq_row = jnp.tile(inv_freq, LANES // half).reshape(1, LANES)
    lane = np.arange(LANES)
    sign_row = jnp.asarray(np.where(lane % HEAD_DIM < half, -1.0, 1.0), F32).reshape(1, LANES)
    rows = 1024
    return pl.pallas_call(
        _rope_table_kernel,
        grid=(n // rows,),
        in_specs=[pl.BlockSpec((rows, 1), lambda i: (i, 0)), _resident(), _resident()],
        out_specs=[pl.BlockSpec((rows, LANES), lambda i: (i, 0))] * 2,
        out_shape=[jax.ShapeDtypeStruct((n, LANES), F32)] * 2,
        compiler_params=_compiler_params(1),
        name="rope_tables",
    )(positions.reshape(n, 1), freq_row, sign_row)


def _rope128(x, cos, sin_signed, lower_half):
    partner = jnp.where(lower_half, pltpu.roll(x, LANES - HEAD_DIM // 2, 1), pltpu.roll(x, HEAD_DIM // 2, 1))
    return x * cos + partner * sin_signed


def _in_kernel(tiles_per_seq, entry_ln, h_ref, lng_ref, lnb_ref, cos_ref, sin_ref, wa_ref, ba_ref,
               poolw_ref, pscale_ref, dww_ref, dwb_ref, clng_ref, clnb_ref,
               q_ref, kv_ref, qi_ref, ki_ref, wi_ref, ypre_ref, cpre_ref, *rest):
    h_out_ref, pool_scr, conv_scr = rest if entry_ln else (None,) + rest
    i = pl.program_id(0)
    seq_tile = lax.rem(i, tiles_per_seq)
    first = seq_tile == 0
    h = h_ref[...]
    if entry_ln:
        h = _layer_norm(h, lng_ref[...], lnb_ref[...])
        h_out_ref[...] = h
    hb = h.astype(BF16)
    cos = cos_ref[...]
    sin = sin_ref[...]
    lane = lax.broadcasted_iota(jnp.int32, (TM, LANES), 1)
    lower_half = (lane & (HEAD_DIM - 1)) < (HEAD_DIM // 2)

    @pl.when(first)
    def _():
        pool_scr[...] = jnp.zeros_like(pool_scr)
        conv_scr[TM:TM + CONV_HALO, :] = jnp.zeros((CONV_HALO, D_CONV), F32)

    def proj(seg):
        lo, hi = seg
        return jnp.dot(hb, wa_ref[:, lo:hi], preferred_element_type=F32) + ba_ref[:, lo:hi]

    zq = proj(SEG_Q)
    for c in range(D_ATTN // LANES):
        blk = zq[:, c * LANES:(c + 1) * LANES]
        q_ref[:, c * LANES:(c + 1) * LANES] = (_rope128(blk, cos, sin, lower_half) * Q_SCALE).astype(BF16)
    zkv = proj(SEG_KV)
    kv_ref[:, 0:LANES] = _rope128(zkv[:, 0:LANES], cos, sin, lower_half).astype(BF16)
    kv_ref[:, LANES:2 * LANES] = zkv[:, LANES:2 * LANES].astype(BF16)
    zqi = proj(SEG_QI)
    for c in range(IDX_HEADS * IDX_DIM // LANES):
        blk = zqi[:, c * LANES:(c + 1) * LANES]
        qi_ref[:, c * LANES:(c + 1) * LANES] = (
            _rope128(blk, cos, sin, lower_half) * (IDX_DIM ** -0.5)).astype(BF16)
    ki_ref[...] = _rope128(proj(SEG_KI), cos, sin, lower_half).astype(BF16)
    wi_t = jnp.transpose(proj(SEG_WI) * (IDX_HEADS ** -0.5))
    wi_ref[...] = wi_t[0:WI_ROWS, :]

    u = proj(SEG_POOL)
    ext = jnp.concatenate([pool_scr[...], u], axis=0)
    pool_scr[...] = u[TM - POOL_HALO:, :]
    s2 = ext + pltpu.roll(ext, 1, 0)
    s4 = s2 + pltpu.roll(s2, 2, 0)
    s8 = s4 + pltpu.roll(s4, 4, 0)
    s16 = s8 + pltpu.roll(s8, 8, 0)
    row = lax.broadcasted_iota(jnp.int32, (TM, D_POOL), 0) + seq_tile * TM
    lane_p = lax.broadcasted_iota(jnp.int32, (TM, D_POOL), 1)
    grp = lane_p >> 6
    win = jnp.where(grp == 0, POOL_WINDOWS[0],
                    jnp.where(grp == 1, POOL_WINDOWS[1],
                              jnp.where(grp == 2, POOL_WINDOWS[2], POOL_WINDOWS[3])))
    cnt = jnp.minimum(row + 1, win).astype(F32)
    wsum = jnp.where(grp == 0, s2[POOL_HALO:], jnp.where(grp == 1, s4[POOL_HALO:],
                     jnp.where(grp == 2, s8[POOL_HALO:], s16[POOL_HALO:])))
    mixed = wsum / cnt - u
    y = jnp.dot(mixed.astype(BF16), poolw_ref[...], preferred_element_type=F32)
    ypre_ref[...] = (y * pscale_ref[...]).astype(BF16)

    uc = proj(SEG_CONV)
    glu = uc[:, :D_CONV] * jax.nn.sigmoid(uc[:, D_CONV:])
    conv_scr[0:CONV_HALO, :] = conv_scr[TM:TM + CONV_HALO, :]
    conv_scr[CONV_HALO:CONV_HALO + TM, :] = glu
    acc = jnp.zeros((TM, D_CONV), F32) + dwb_ref[...]
    for j in range(CONV_WIDTH):
        start = CONV_HALO - (CONV_WIDTH - 1) + j
        acc = acc + dww_ref[j:j + 1, :] * conv_scr[start:start + TM, :]
    hc = _layer_norm(acc, clng_ref[...], clnb_ref[...])
    cpre_ref[...] = (hc * jax.nn.sigmoid(hc)).astype(BF16)


def _in_call(h, entry_ln, lng, lnb, cos_t, sin_t, wa, ba, poolw, pscale, dww, dwb, clng, clnb, seq_len):
    n = h.shape[0]
    tiles_per_seq = seq_len // TM
    row_spec = lambda w: pl.BlockSpec((TM, w), lambda i: (i, 0))
    out_widths = (D_ATTN, 2 * LANES, IDX_HEADS * IDX_DIM, LANES, D_POOL, D_CONV)
    out_dtypes = (BF16, BF16, BF16, BF16, BF16, BF16)
    out_specs = [row_spec(w) for w in out_widths]
    out_shape = [jax.ShapeDtypeStruct((n, w), d) for w, d in zip(out_widths, out_dtypes)]
    out_specs.insert(4, pl.BlockSpec((WI_ROWS, TM), lambda i: (0, i)))
    out_shape.insert(4, jax.ShapeDtypeStruct((WI_ROWS, n), F32))
    if entry_ln:
        out_specs.append(row_spec(D_MODEL))
        out_shape.append(jax.ShapeDtypeStruct((n, D_MODEL), F32))
    return pl.pallas_call(
        functools.partial(_in_kernel, tiles_per_seq, entry_ln),
        grid=(n // TM,),
        in_specs=([row_spec(D_MODEL), _resident(), _resident(), row_spec(LANES), row_spec(LANES)]
                  + [_resident()] * 8),
        out_specs=out_specs,
        out_shape=out_shape,
        scratch_shapes=[pltpu.VMEM((POOL_HALO, D_POOL), F32),
                        pltpu.VMEM((TM + CONV_HALO, D_CONV), F32)],
        compiler_params=_compiler_params(1),
        name="in_proj",
    )(h, lng, lnb, cos_t, sin_t, wa, ba, poolw, pscale, dww, dwb, clng, clnb)


def _attn_kernel(topk, seq_len, q_ref, qi_ref, wit_ref, kv_ref, ki_ref, o_ref,
                 key_scr, bias_scr, qi_scr, jmax_scr, hi_scr, lo_scr):
    i = pl.program_id(1)
    n_chunks = (i * TQ + TQ - 1) // KC + 1
    lane = lax.broadcasted_iota(jnp.int32, (TQ, LANES), 1)
    lower = lane < HEAD_DIM
    nt_dims = (((1,), (1,)), ((), ()))
    zero_b = jnp.zeros((TQ, LANES), BF16)
    kpos_t = lax.broadcasted_iota(jnp.int32, (KC, TQ), 0)
    qpos_t = i * TQ + lax.broadcasted_iota(jnp.int32, (KC, TQ), 1)
    qpos_row = i * TQ + lax.broadcasted_iota(jnp.int32, (1, TQ), 1)

    qi = qi_ref[...]
    for h in range(IDX_HEADS):
        blk = qi[:, (h // 2) * LANES:(h // 2 + 1) * LANES]
        qi_scr[h] = jnp.where(lower if h % 2 == 0 else jnp.logical_not(lower), blk, zero_b)
    wit = wit_ref[...]

    def score_chunk(kc, carry):
        k0 = pl.multiple_of(kc * KC, KC)
        kic = ki_ref[pl.ds(k0, KC), :]
        acc = jnp.zeros((KC, TQ), F32)
        for h in range(IDX_HEADS):
            logits = lax.dot_general(kic, qi_scr[h], nt_dims, preferred_element_type=F32)
            acc = acc + jnp.maximum(logits, 0.0) * wit[h:h + 1, :]
        bits = lax.bitcast_convert_type(acc, jnp.int32)
        key = bits ^ ((bits >> 31) & 0x7FFFFFFF)
        key = jnp.where(kpos_t + k0 <= qpos_t, key, INT_MIN)
        key_scr[kc] = key
        hi_scr[kc] = (key >> 16).astype(jnp.int16)
        lo_scr[kc] = ((key & 0xFFFF) - HALF_BIAS).astype(jnp.int16)
        return carry

    lax.fori_loop(0, n_chunks, score_chunk, 0)

    def count(pred):
        def body(kc, acc):
            hit = pred(key_scr[kc], kc)
            ones = jnp.where(hit, 1.0, 0.0)
            return acc + jnp.sum(ones.reshape(KC // COUNT_ROWS, COUNT_ROWS, TQ), axis=0)
        acc = lax.fori_loop(0, n_chunks, body, jnp.zeros((COUNT_ROWS, TQ), F32))
        return jnp.sum(acc, axis=0, keepdims=True)

    def count16(src_scr, pred):
        def body(kc, acc):
            ones = jnp.where(pred(src_scr[kc]), jnp.int16(1), jnp.int16(0))
            for r0 in range(0, KC, COUNT_ROWS):
                acc = acc + ones[r0:r0 + COUNT_ROWS]
            return acc
        acc = lax.fori_loop(0, n_chunks, body, jnp.zeros((COUNT_ROWS, TQ), jnp.int16))
        return jnp.sum(acc.astype(F32), axis=0, keepdims=True)

    kf = float(topk)

    def search16(src_scr, base):
        c0 = count16(src_scr, lambda x: x >= 0)
        t0 = jnp.where(base + c0 >= kf, 0, -HALF_BIAS).astype(jnp.int32)

        def step(it, t):
            cand = t + (jnp.int32(1) << (14 - it))
            cand16 = cand.astype(jnp.int16)
            c = count16(src_scr, lambda x: x >= cand16)
            return jnp.where(base + c >= kf, cand, t)

        return lax.fori_loop(0, 15, step, t0)

    thr_hi = search16(hi_scr, 0.0)
    thr_hi16 = thr_hi.astype(jnp.int16)

    def mask_lo(kc, carry):
        lo_scr[kc] = jnp.where(hi_scr[kc] == thr_hi16, lo_scr[kc], jnp.int16(-HALF_BIAS))
        return carry

    lax.fori_loop(0, n_chunks, mask_lo, 0)
    thr_lo = search16(lo_scr, count16(hi_scr, lambda x: x > thr_hi16))
    thr = thr_hi * (2 * HALF_BIAS) + (thr_lo + HALF_BIAS)

    cnt_ge = count(lambda key, kc: key >= thr)
    need = kf - count(lambda key, kc: key > thr)
    jmax_scr[...] = jnp.full((SUBLANES, TQ), seq_len, jnp.int32)

    @pl.when(jnp.max(jnp.where(thr == INT_MIN, 0.0, cnt_ge)) > kf)
    def _():
        tie_bits = (seq_len - 1).bit_length()

        def tie_step(it, jmax):
            cand = jmax + (jnp.int32(1) << (tie_bits - 1 - it))
            c = count(lambda key, kc: (key == thr) & (kpos_t + kc * KC < cand))
            return jnp.where(c < need, cand, jmax)

        jmax = lax.fori_loop(0, tie_bits, tie_step, jnp.zeros((1, TQ), jnp.int32))
        jmax_scr[...] = jnp.broadcast_to(jmax, (SUBLANES, TQ))

    jmax = jnp.where(thr == INT_MIN, qpos_row, jmax_scr[0:1, :])

    def bias_chunk(kc, carry):
        key = key_scr[kc]
        kp = kpos_t + kc * KC
        sel = (key > thr) | ((key == thr) & (kp <= jmax))
        bias_scr[kc] = jnp.transpose(jnp.where(sel, 0.0, NEG_BIG))
        return carry

    lax.fori_loop(0, n_chunks, bias_chunk, 0)

    qv = q_ref[...]
    rows = (N_HEADS // N_KV_HEADS) * TQ
    qs = []
    for n in range(N_KV_HEADS):
        keep = lower if n == 0 else jnp.logical_not(lower)
        parts = [jnp.where(keep, qv[:, c * LANES:(c + 1) * LANES], zero_b) for c in range(D_ATTN // LANES)]
        qs.append(jnp.concatenate(parts, axis=0))

    def attn_chunk(kc, carry):
        k0 = pl.multiple_of(kc * KC, KC)
        kvc = kv_ref[pl.ds(k0, KC), :]
        kch = kvc[:, 0:LANES]
        vch = kvc[:, LANES:2 * LANES]
        bias = bias_scr[kc]
        bias4 = jnp.concatenate([bias] * (N_HEADS // N_KV_HEADS), axis=0)
        lane_k = lax.broadcasted_iota(jnp.int32, (KC, LANES), 1)
        out = []
        for n in range(N_KV_HEADS):
            m, acc = carry[n]
            own = (lane_k < HEAD_DIM) if n == 0 else (lane_k >= HEAD_DIM)
            v_aug = jnp.where(own, vch, jnp.ones_like(vch))
            s = lax.dot_general(qs[n], kch, nt_dims, preferred_element_type=F32) + bias4
            m_new = jnp.maximum(m, jnp.max(s, axis=-1, keepdims=True))
            p = jnp.exp2((s - m_new).astype(BF16))
            acc_new = jnp.exp2(m - m_new) * acc + jnp.dot(p, v_aug, preferred_element_type=F32)
            out.append((m_new, acc_new))
        return tuple(out)

    init = tuple((jnp.full((rows, 1), NEG_BIG, F32), jnp.zeros((rows, LANES), F32))
                 for _ in range(N_KV_HEADS))
    res = lax.fori_loop(0, n_chunks, attn_chunk, init)
    o0 = res[0][1] / pltpu.roll(res[0][1], HEAD_DIM, 1)
    o1 = res[1][1] / pltpu.roll(res[1][1], HEAD_DIM, 1)
    for c in range(D_ATTN // LANES):
        blk = jnp.where(lower, o0[c * TQ:(c + 1) * TQ], o1[c * TQ:(c + 1) * TQ])
        o_ref[:, c * LANES:(c + 1) * LANES] = blk.astype(BF16)


def _attn_call(q, qi, wi, kv, ki, batch, seq_len):
    n = q.shape[0]
    nq = seq_len // TQ
    topk = min(TOPK_MAX, seq_len // 4)
    qspec = lambda w: pl.BlockSpec((TQ, w), lambda b, i: (b * nq + i, 0))
    sspec = lambda w: pl.BlockSpec((seq_len, w), lambda b, i: (b, 0))
    wspec = pl.BlockSpec((WI_ROWS, TQ), lambda b, i: (0, b * nq + i))
    return pl.pallas_call(
        functools.partial(_attn_kernel, topk, seq_len),
        grid=(batch, nq),
        in_specs=[qspec(D_ATTN), qspec(IDX_HEADS * IDX_DIM), wspec, sspec(2 * LANES), sspec(LANES)],
        out_specs=qspec(D_ATTN),
        out_shape=jax.ShapeDtypeStruct((n, D_ATTN), BF16),
        scratch_shapes=[pltpu.VMEM((seq_len // KC, KC, TQ), jnp.int32),
                        pltpu.VMEM((seq_len // KC, TQ, KC), F32),
                        pltpu.VMEM((IDX_HEADS, TQ, LANES), BF16),
                        pltpu.VMEM((SUBLANES, TQ), jnp.int32),
                        pltpu.VMEM((seq_len // KC, KC, TQ), jnp.int16),
                        pltpu.VMEM((seq_len // KC, KC, TQ), jnp.int16)],
        compiler_params=_compiler_params(2),
        name="dsa_attn",
    )(q, qi, wi, kv, ki)


def _mix_kernel(alpha, h_ref, o_ref, ypre_ref, cpre_ref, wg_ref, bg_ref, wpo_ref, wco_ref, wao_ref,
                wo_ref, g_ref, b_ref, out_ref):
    h = h_ref[...]
    hb = h.astype(BF16)
    branches = ((ypre_ref, wpo_ref), (cpre_ref, wco_ref), (o_ref, wao_ref))
    merged = jnp.zeros((TM, D_MODEL), F32)
    for n, (x_ref, w_ref) in enumerate(branches):
        gate = jnp.dot(hb, wg_ref[:, n * D_MODEL:(n + 1) * D_MODEL], preferred_element_type=F32)
        gate = jax.nn.sigmoid(gate + bg_ref[:, n * D_MODEL:(n + 1) * D_MODEL])
        merged = merged + gate * jnp.dot(x_ref[...], w_ref[...], preferred_element_type=F32)
    mix = jnp.dot(merged.astype(BF16), wo_ref[...], preferred_element_type=F32)
    out_ref[...] = _layer_norm(alpha * h + mix, g_ref[...], b_ref[...])


def _mix_call(alpha, h, o, ypre, cpre, wg, bg, wpo, wco, wao, wo, g, b):
    n = h.shape[0]
    row_spec = lambda w: pl.BlockSpec((TM, w), lambda i: (i, 0))
    return pl.pallas_call(
        functools.partial(_mix_kernel, alpha),
        grid=(n // TM,),
        in_specs=[row_spec(D_MODEL), row_spec(D_ATTN), row_spec(D_POOL), row_spec(D_CONV)] + [_resident()] * 8,
        out_specs=row_spec(D_MODEL),
        out_shape=jax.ShapeDtypeStruct((n, D_MODEL), F32),
        compiler_params=_compiler_params(1),
        name="gated_merge",
    )(h, o, ypre, cpre, wg, bg, wpo, wco, wao, wo, g, b)


def _ffn_kernel(alpha, tiles_per_seq, h_ref, wug_ref, wuv_ref, dwg_ref, dwv_ref, dbg_ref, dbv_ref,
                wd_ref, g_ref, b_ref, out_ref, carry_g, carry_v, act_ref):
    i = pl.program_id(0)
    first = lax.rem(i, tiles_per_seq) == 0
    h = h_ref[...]
    hb = h.astype(BF16)

    @pl.when(first)
    def _():
        carry_g[...] = jnp.zeros_like(carry_g)
        carry_v[...] = jnp.zeros_like(carry_v)

    def conv3(u, carry_ref, c, w_ref, b_ref2):
        prev = carry_ref[c]
        carry_ref[c] = u[TM - FFN_HALO:, :]
        ext = jnp.concatenate([prev, u], axis=0)
        w = w_ref[c]
        y = (w[2:3, :] * u + w[1:2, :] * pltpu.roll(ext, 1, 0)[FFN_HALO:]
             + w[0:1, :] * pltpu.roll(ext, 2, 0)[FFN_HALO:])
        return y + b_ref2[c]

    def up(c):
        return (jnp.dot(hb, wug_ref[c], preferred_element_type=F32),
                jnp.dot(hb, wuv_ref[c], preferred_element_type=F32))

    nc = D_FF // FC
    u_next = up(0)
    for c in range(nc):
        ug, uv = u_next
        if c + 1 < nc:
            u_next = up(c + 1)
        gate = conv3(ug, carry_g, c, dwg_ref, dbg_ref)
        val = conv3(uv, carry_v, c, dwv_ref, dbv_ref)
        act_ref[:, c * FC:(c + 1) * FC] = (gate * jax.nn.sigmoid(gate) * val).astype(BF16)
    ffn = jnp.dot(act_ref[...], wd_ref[...], preferred_element_type=F32)
    out_ref[...] = _layer_norm(alpha * h + ffn, g_ref[...], b_ref[...])


def _ffn_call(alpha, h, wug, wuv, dwg, dwv, dbg, dbv, wd, g, b, seq_len):
    n = h.shape[0]
    nc = D_FF // FC
    row_spec = pl.BlockSpec((TM, D_MODEL), lambda i: (i, 0))
    return pl.pallas_call(
        functools.partial(_ffn_kernel, alpha, seq_len // TM),
        grid=(n // TM,),
        in_specs=[row_spec] + [_resident()] * 9,
        out_specs=row_spec,
        out_shape=jax.ShapeDtypeStruct((n, D_MODEL), F32),
        scratch_shapes=[pltpu.VMEM((nc, FFN_HALO, FC), F32), pltpu.VMEM((nc, FFN_HALO, FC), F32),
                        pltpu.VMEM((TM, D_FF), BF16)],
        compiler_params=_compiler_params(1),
        name="conv_ffn",
    )(h, wug, wuv, dwg, dwv, dbg, dbv, wd, g, b)


def _head_pair_perm():
    group = N_HEADS // N_KV_HEADS
    cols = []
    for j in range(group):
        cols.extend(range(j * HEAD_DIM, (j + 1) * HEAD_DIM))
        cols.extend(range((group + j) * HEAD_DIM, (group + j + 1) * HEAD_DIM))
    return np.asarray(cols)


def _prep_in_weights(w_in, b_in):
    sizes = (D_POOL, 2 * D_CONV, D_ATTN, D_KV, D_KV, IDX_HEADS * IDX_DIM, IDX_DIM, IDX_HEADS, N_BRANCH * D_MODEL)
    offs = np.concatenate([[0], np.cumsum(sizes)])
    seg = lambda a, k: a[..., offs[k]:offs[k + 1]]
    perm = _head_pair_perm()

    def relayout(a):
        pad = jnp.zeros(a.shape[:-1] + (LANES - IDX_HEADS,), a.dtype)
        return jnp.concatenate([seg(a, 0), seg(a, 1), seg(a, 2)[..., perm], seg(a, 3), seg(a, 4), seg(a, 5),
                                seg(a, 6), seg(a, 6), seg(a, 7), pad], axis=-1)

    wa = relayout(w_in).astype(BF16)
    ba = relayout(b_in[None, :])
    wg = seg(w_in, 8).astype(BF16)
    bg = seg(b_in[None, :], 8)
    return wa, ba, wg, bg


def _block_diag(pool_w):
    out = jnp.zeros((D_POOL, D_POOL), pool_w.dtype)
    for g in range(POOL_GROUPS):
        lo = g * POOL_GROUP_DIM
        out = out.at[lo:lo + POOL_GROUP_DIM, lo:lo + POOL_GROUP_DIM].set(pool_w[g])
    return out


def kernel(x, positions, ln_in_g, ln_in_b, w_in, b_in, pool_w, pool_scale, w_pool_out, conv_dw_w, conv_dw_b,
           conv_ln_g, conv_ln_b, w_conv_out, w_attn_out, w_o, ln1_g, ln1_b, w_up, ffn_dw_w, ffn_dw_b, w_down,
           ln2_g, ln2_b):
    batch, seq_len, d_model = x.shape
    depth = w_in.shape[0]
    assert d_model == D_MODEL and seq_len % TM == 0 and seq_len % KC == 0 and KC % TQ == 0
    n = batch * seq_len
    alpha = float((2 * depth) ** 0.25)
    nc = D_FF // FC
    perm = _head_pair_perm()
    row = lambda a: a.reshape(1, -1)

    cos_t, sin_t = _rope_tables(positions)
    h = x.reshape(n, D_MODEL)
    for l in range(depth):
        wa, ba, wg, bg = _prep_in_weights(w_in[l], b_in[l])
        outs = _in_call(
            h, l == 0, row(ln_in_g), row(ln_in_b), cos_t, sin_t, wa, ba,
            _block_diag(pool_w[l]).astype(BF16), row(pool_scale[l]),
            conv_dw_w[l], row(conv_dw_b[l]), row(conv_ln_g[l]), row(conv_ln_b[l]), seq_len)
        q, kv, qi, ki, wi, ypre, cpre = outs[:7]
        if l == 0:
            h = outs[7]
        o = _attn_call(q, qi, wi, kv, ki, batch, seq_len)
        h = _mix_call(alpha, h, o, ypre, cpre, wg, bg, w_pool_out[l].astype(BF16), w_conv_out[l].astype(BF16),
                      w_attn_out[l][perm].astype(BF16), w_o[l].astype(BF16), row(ln1_g[l]), row(ln1_b[l]))
        split = lambda a, lo: a[..., lo:lo + D_FF].reshape(a.shape[:-1] + (nc, FC))
        wug = jnp.moveaxis(split(w_up[l], 0), 1, 0).astype(BF16)
        wuv = jnp.moveaxis(split(w_up[l], D_FF), 1, 0).astype(BF16)
        dwg = jnp.moveaxis(split(ffn_dw_w[l], 0), 1, 0)
        dwv = jnp.moveaxis(split(ffn_dw_w[l], D_FF), 1, 0)
        dbg = split(ffn_dw_b[l], 0).reshape(nc, 1, FC)
        dbv = split(ffn_dw_b[l], D_FF).reshape(nc, 1, FC)
        wd = w_down[l].astype(BF16)
        h = _ffn_call(alpha, h, wug, wuv, dwg, dwv, dbg, dbv, wd, row(ln2_g[l]), row(ln2_b[l]), seq_len)
    return h.reshape(batch, seq_len, D_MODEL)
```

```python
import functools

import jax
import jax.numpy as jnp
import numpy as np
from jax import lax
from jax.experimental import pallas as pl
from jax.experimental.pallas import tpu as pltpu

F32 = jnp.float32
BF16 = jnp.bfloat16

D_MODEL = 1024
POOL_GROUPS = 4
POOL_GROUP_DIM = 64
POOL_WINDOWS = (2, 4, 8, 16)
D_POOL = POOL_GROUPS * POOL_GROUP_DIM
D_CONV = 256
CONV_WIDTH = 31
N_HEADS = 8
N_KV_HEADS = 2
HEAD_DIM = 64
D_ATTN = N_HEADS * HEAD_DIM
D_KV = N_KV_HEADS * HEAD_DIM
IDX_HEADS = 8
IDX_DIM = 64
TOPK_MAX = 256
ROPE_THETA = 10000.0
N_BRANCH = 3
D_FF = 2816
FFN_CONV_WIDTH = 3
LN_EPS = 1e-5

LANES = 128
SUBLANES = 8
VMEM_LIMIT_BYTES = 56 * 1024 * 1024

TM = 512
TQ = 256
COUNT_ROWS = 32
WI_ROWS = 8
KC = 512
FC = 256
POOL_HALO = 16
CONV_HALO = 32
FFN_HALO = 8

SEG_POOL = (0, 256)
SEG_CONV = (256, 768)
SEG_Q = (768, 1280)
SEG_KV = (1280, 1536)
SEG_QI = (1536, 2048)
SEG_KI = (2048, 2176)
SEG_WI = (2176, 2304)
D_A = 2304

Q_SCALE = float(HEAD_DIM ** -0.5 * np.log2(np.e))
INT_MIN = -(2 ** 31)
HALF_BIAS = 2 ** 15
NEG_BIG = -1e30


def _layer_norm(x, g, b):
    mu = jnp.mean(x, axis=-1, keepdims=True)
    xc = x - mu
    var = jnp.mean(xc * xc, axis=-1, keepdims=True)
    return xc * lax.rsqrt(var + LN_EPS) * g + b


def _compiler_params(n_axes):
    return pltpu.CompilerParams(
        dimension_semantics=("arbitrary",) * n_axes,
        vmem_limit_bytes=VMEM_LIMIT_BYTES,
    )


def _resident():
    return pl.BlockSpec(memory_space=pltpu.VMEM)


def _rope_table_kernel(pos_ref, freq_ref, sign_ref, cos_ref, sin_ref):
    ang = pos_ref[...].astype(F32) * freq_ref[...]
    cos_ref[...] = jnp.cos(ang)
    sin_ref[...] = jnp.sin(ang) * sign_ref[...]


def _rope_tables(positions):
    n = positions.size
    half = HEAD_DIM // 2
    inv_freq = ROPE_THETA ** (-jnp.arange(half, dtype=F32) / half)
    freq_row = jnp.tile(inv_freq, LANES // half).reshape(1, LANES)
    lane = np.arange(LANES)
    sign_row = jnp.asarray(np.where(lane % HEAD_DIM < half, -1.0, 1.0), F32).reshape(1, LANES)
    rows = 1024
    return pl.pallas_call(
        _rope_table_kernel,
        grid=(n // rows,),
        in_specs=[pl.BlockSpec((rows, 1), lambda i: (i, 0)), _resident(), _resident()],
        out_specs=[pl.BlockSpec((rows, LANES), lambda i: (i, 0))] * 2,
        out_shape=[jax.ShapeDtypeStruct((n, LANES), F32)] * 2,
        compiler_params=_compiler_params(1),
        name="rope_tables",
    )(positions.reshape(n, 1), freq_row, sign_row)


def _rope128(x, cos, sin_signed, lower_half):
    partner = jnp.where(lower_half, pltpu.roll(x, LANES - HEAD_DIM // 2, 1), pltpu.roll(x, HEAD_DIM // 2, 1))
    return x * cos + partner * sin_signed


def _in_kernel(tiles_per_seq, entry_ln, h_ref, lng_ref, lnb_ref, cos_ref, sin_ref, wa_ref, ba_ref,
               poolw_ref, pscale_ref, dww_ref, dwb_ref, clng_ref, clnb_ref,
               q_ref, kv_ref, qi_ref, ki_ref, wi_ref, ypre_ref, cpre_ref, *rest):
    h_out_ref, pool_scr, conv_scr = rest if entry_ln else (None,) + rest
    i = pl.program_id(0)
    seq_tile = lax.rem(i, tiles_per_seq)
    first = seq_tile == 0
    h = h_ref[...]
    if entry_ln:
        h = _layer_norm(h, lng_ref[...], lnb_ref[...])
        h_out_ref[...] = h
    hb = h.astype(BF16)
    cos = cos_ref[...]
    sin = sin_ref[...]
    lane = lax.broadcasted_iota(jnp.int32, (TM, LANES), 1)
    lower_half = (lane & (HEAD_DIM - 1)) < (HEAD_DIM // 2)

    @pl.when(first)
    def _():
        pool_scr[...] = jnp.zeros_like(pool_scr)
        conv_scr[TM:TM + CONV_HALO, :] = jnp.zeros((CONV_HALO, D_CONV), F32)

    def proj(seg):
        lo, hi = seg
        return jnp.dot(hb, wa_ref[:, lo:hi], preferred_element_type=F32) + ba_ref[:, lo:hi]

    zq = proj(SEG_Q)
    for c in range(D_ATTN // LANES):
        blk = zq[:, c * LANES:(c + 1) * LANES]
        q_ref[:, c * LANES:(c + 1) * LANES] = (_rope128(blk, cos, sin, lower_half) * Q_SCALE).astype(BF16)
    zkv = proj(SEG_KV)
    kv_ref[:, 0:LANES] = _rope128(zkv[:, 0:LANES], cos, sin, lower_half).astype(BF16)
    kv_ref[:, LANES:2 * LANES] = zkv[:, LANES:2 * LANES].astype(BF16)
    zqi = proj(SEG_QI)
    for c in range(IDX_HEADS * IDX_DIM // LANES):
        blk = zqi[:, c * LANES:(c + 1) * LANES]
        qi_ref[:, c * LANES:(c + 1) * LANES] = (
            _rope128(blk, cos, sin, lower_half) * (IDX_DIM ** -0.5)).astype(BF16)
    ki_ref[...] = _rope128(proj(SEG_KI), cos, sin, lower_half).astype(BF16)
    wi_t = jnp.transpose(proj(SEG_WI) * (IDX_HEADS ** -0.5))
    wi_ref[...] = wi_t[0:WI_ROWS, :]

    u = proj(SEG_POOL)
    ext = jnp.concatenate([pool_scr[...], u], axis=0)
    pool_scr[...] = u[TM - POOL_HALO:, :]
    s2 = ext + pltpu.roll(ext, 1, 0)
    s4 = s2 + pltpu.roll(s2, 2, 0)
    s8 = s4 + pltpu.roll(s4, 4, 0)
    s16 = s8 + pltpu.roll(s8, 8, 0)
    row = lax.broadcasted_iota(jnp.int32, (TM, D_POOL), 0) + seq_tile * TM
    lane_p = lax.broadcasted_iota(jnp.int32, (TM, D_POOL), 1)
    grp = lane_p >> 6
    win = jnp.where(grp == 0, POOL_WINDOWS[0],
                    jnp.where(grp == 1, POOL_WINDOWS[1],
                              jnp.where(grp == 2, POOL_WINDOWS[2], POOL_WINDOWS[3])))
    cnt = jnp.minimum(row + 1, win).astype(F32)
    wsum = jnp.where(grp == 0, s2[POOL_HALO:], jnp.where(grp == 1, s4[POOL_HALO:],
                     jnp.where(grp == 2, s8[POOL_HALO:], s16[POOL_HALO:])))
    mixed = wsum / cnt - u
    y = jnp.dot(mixed.astype(BF16), poolw_ref[...], preferred_element_type=F32)
    ypre_ref[...] = (y * pscale_ref[...]).astype(BF16)

    uc = proj(SEG_CONV)
    glu = uc[:, :D_CONV] * jax.nn.sigmoid(uc[:, D_CONV:])
    conv_scr[0:CONV_HALO, :] = conv_scr[TM:TM + CONV_HALO, :]
    conv_scr[CONV_HALO:CONV_HALO + TM, :] = glu
    acc = jnp.zeros((TM, D_CONV), F32) + dwb_ref[...]
    ext_c = conv_scr[...]
    for b in range(SUBLANES):
        shifted = ext_c if b == 0 else pltpu.roll(ext_c, b, 0)
        for a in range(CONV_HALO // SUBLANES):
            d = SUBLANES * a + b
            if d < CONV_WIDTH:
                lo = CONV_HALO - SUBLANES * a
                acc = acc + dww_ref[CONV_WIDTH - 1 - d:CONV_WIDTH - d, :] * shifted[lo:lo + TM, :]
    hc = _layer_norm(acc, clng_ref[...], clnb_ref[...])
    cpre_ref[...] = (hc * jax.nn.sigmoid(hc)).astype(BF16)


def _in_call(h, entry_ln, lng, lnb, cos_t, sin_t, wa, ba, poolw, pscale, dww, dwb, clng, clnb, seq_len):
    n = h.shape[0]
    tiles_per_seq = seq_len // TM
    row_spec = lambda w: pl.BlockSpec((TM, w), lambda i: (i, 0))
    out_widths = (D_ATTN, 2 * LANES, IDX_HEADS * IDX_DIM, LANES, D_POOL, D_CONV)
    out_dtypes = (BF16, BF16, BF16, BF16, BF16, BF16)
    out_specs = [row_spec(w) for w in out_widths]
    out_shape = [jax.ShapeDtypeStruct((n, w), d) for w, d in zip(out_widths, out_dtypes)]
    out_specs.insert(4, pl.BlockSpec((WI_ROWS, TM), lambda i: (0, i)))
    out_shape.insert(4, jax.ShapeDtypeStruct((WI_ROWS, n), F32))
    if entry_ln:
        out_specs.append(row_spec(D_MODEL))
        out_shape.append(jax.ShapeDtypeStruct((n, D_MODEL), F32))
    return pl.pallas_call(
        functools.partial(_in_kernel, tiles_per_seq, entry_ln),
        grid=(n // TM,),
        in_specs=([row_spec(D_MODEL), _resident(), _resident(), row_spec(LANES), row_spec(LANES)]
                  + [_resident()] * 8),
        out_specs=out_specs,
        out_shape=out_shape,
        scratch_shapes=[pltpu.VMEM((POOL_HALO, D_POOL), F32),
                        pltpu.VMEM((TM + CONV_HALO, D_CONV), F32)],
        compiler_params=_compiler_params(1),
        name="in_proj",
    )(h, lng, lnb, cos_t, sin_t, wa, ba, poolw, pscale, dww, dwb, clng, clnb)


def _attn_kernel(topk, seq_len, q_ref, qi_ref, wit_ref, kv_ref, ki_ref, o_ref,
                 key_scr, bias_scr, qi_scr, jmax_scr, hi_scr, lo_scr):
    i = pl.program_id(1)
    n_chunks = (i * TQ + TQ - 1) // KC + 1
    lane = lax.broadcasted_iota(jnp.int32, (TQ, LANES), 1)
    lower = lane < HEAD_DIM
    nt_dims = (((1,), (1,)), ((), ()))
    zero_b = jnp.zeros((TQ, LANES), BF16)
    kpos_t = lax.broadcasted_iota(jnp.int32, (KC, TQ), 0)
    qpos_t = i * TQ + lax.broadcasted_iota(jnp.int32, (KC, TQ), 1)
    qpos_row = i * TQ + lax.broadcasted_iota(jnp.int32, (1, TQ), 1)

    qi = qi_ref[...]
    for h in range(IDX_HEADS):
        blk = qi[:, (h // 2) * LANES:(h // 2 + 1) * LANES]
        qi_scr[h] = jnp.where(lower if h % 2 == 0 else jnp.logical_not(lower), blk, zero_b)
    wit = wit_ref[...]

    def score_chunk(kc, carry):
        k0 = pl.multiple_of(kc * KC, KC)
        kic = ki_ref[pl.ds(k0, KC), :]
        acc = jnp.zeros((KC, TQ), F32)
        for h in range(IDX_HEADS):
            logits = lax.dot_general(kic, qi_scr[h], nt_dims, preferred_element_type=F32)
            acc = acc + jnp.maximum(logits, 0.0) * wit[h:h + 1, :]
        bits = lax.bitcast_convert_type(acc, jnp.int32)
        key = bits ^ ((bits >> 31) & 0x7FFFFFFF)
        key = jnp.where(kpos_t + k0 <= qpos_t, key, INT_MIN)
        key_scr[kc] = key
        hi_scr[kc] = (key >> 16).astype(jnp.int16)
        lo_scr[kc] = ((key & 0xFFFF) - HALF_BIAS).astype(jnp.int16)
        return carry

    lax.fori_loop(0, n_chunks, score_chunk, 0)

    def count(pred):
        def body(kc, acc):
            hit = pred(key_scr[kc], kc)
            ones = jnp.where(hit, 1.0, 0.0)
            return acc + jnp.sum(ones.reshape(KC // COUNT_ROWS, COUNT_ROWS, TQ), axis=0)
        acc = lax.fori_loop(0, n_chunks, body, jnp.zeros((COUNT_ROWS, TQ), F32))
        return jnp.sum(acc, axis=0, keepdims=True)

    def count16(src_scr, pred):
        def body(kc, acc):
            ones = jnp.where(pred(src_scr[kc]), jnp.int16(1), jnp.int16(0))
            for r0 in range(0, KC, COUNT_ROWS):
                acc = acc + ones[r0:r0 + COUNT_ROWS]
            return acc
        acc = lax.fori_loop(0, n_chunks, body, jnp.zeros((COUNT_ROWS, TQ), jnp.int16))
        return jnp.sum(acc.astype(F32), axis=0, keepdims=True)

    kf = float(topk)

    def search16(src_scr, base):
        c0 = count16(src_scr, lambda x: x >= 0)
        t0 = jnp.where(base + c0 >= kf, 0, -HALF_BIAS).astype(jnp.int32)

        def step(it, t):
            cand = t + (jnp.int32(1) << (14 - it))
            cand16 = cand.astype(jnp.int16)
            c = count16(src_scr, lambda x: x >= cand16)
            return jnp.where(base + c >= kf, cand, t)

        return lax.fori_loop(0, 15, step, t0)

    thr_hi = search16(hi_scr, 0.0)
    thr_hi16 = thr_hi.astype(jnp.int16)

    def mask_lo(kc, carry):
        lo_scr[kc] = jnp.where(hi_scr[kc] == thr_hi16, lo_scr[kc], jnp.int16(-HALF_BIAS))
        return carry

    lax.fori_loop(0, n_chunks, mask_lo, 0)
    thr_lo = search16(lo_scr, count16(hi_scr, lambda x: x > thr_hi16))
    thr = thr_hi * (2 * HALF_BIAS) + (thr_lo + HALF_BIAS)

    cnt_ge = count(lambda key, kc: key >= thr)
    need = kf - count(lambda key, kc: key > thr)
    jmax_scr[...] = jnp.full((SUBLANES, TQ), seq_len, jnp.int32)

    @pl.when(jnp.max(jnp.where(thr == INT_MIN, 0.0, cnt_ge)) > kf)
    def _():
        tie_bits = (seq_len - 1).bit_length()

        def tie_step(it, jmax):
            cand = jmax + (jnp.int32(1) << (tie_bits - 1 - it))
            c = count(lambda key, kc: (key == thr) & (kpos_t + kc * KC < cand))
            return jnp.where(c < need, cand, jmax)

        jmax = lax.fori_loop(0, tie_bits, tie_step, jnp.zeros((1, TQ), jnp.int32))
        jmax_scr[...] = jnp.broadcast_to(jmax, (SUBLANES, TQ))

    jmax = jnp.where(thr == INT_MIN, qpos_row, jmax_scr[0:1, :])

    def bias_chunk(kc, carry):
        key = key_scr[kc]
        kp = kpos_t + kc * KC
        sel = (key > thr) | ((key == thr) & (kp <= jmax))
        bias_scr[kc] = jnp.transpose(jnp.where(sel, 0.0, NEG_BIG))
        return carry

    lax.fori_loop(0, n_chunks, bias_chunk, 0)

    qv = q_ref[...]
    rows = (N_HEADS // N_KV_HEADS) * TQ
    qs = []
    for n in range(N_KV_HEADS):
        keep = lower if n == 0 else jnp.logical_not(lower)
        parts = [jnp.where(keep, qv[:, c * LANES:(c + 1) * LANES], zero_b) for c in range(D_ATTN // LANES)]
        qs.append(jnp.concatenate(parts, axis=0))

    def attn_chunk(kc, carry):
        k0 = pl.multiple_of(kc * KC, KC)
        kvc = kv_ref[pl.ds(k0, KC), :]
        kch = kvc[:, 0:LANES]
        vch = kvc[:, LANES:2 * LANES]
        bias = bias_scr[kc]
        bias4 = jnp.concatenate([bias] * (N_HEADS // N_KV_HEADS), axis=0)
        lane_k = lax.broadcasted_iota(jnp.int32, (KC, LANES), 1)
        out = []
        for n in range(N_KV_HEADS):
            m, acc = carry[n]
            own = (lane_k < HEAD_DIM) if n == 0 else (lane_k >= HEAD_DIM)
            v_aug = jnp.where(own, vch, jnp.ones_like(vch))
            s = lax.dot_general(qs[n], kch, nt_dims, preferred_element_type=F32) + bias4
            m_new = jnp.maximum(m, jnp.max(s, axis=-1, keepdims=True))
            p = jnp.exp2((s - m_new).astype(BF16))
            acc_new = jnp.exp2(m - m_new) * acc + jnp.dot(p, v_aug, preferred_element_type=F32)
            out.append((m_new, acc_new))
        return tuple(out)

    init = tuple((jnp.full((rows, 1), NEG_BIG, F32), jnp.zeros((rows, LANES), F32))
                 for _ in range(N_KV_HEADS))
    res = lax.fori_loop(0, n_chunks, attn_chunk, init)
    for c in range(D_ATTN // LANES):
        a0 = res[0][1][c * TQ:(c + 1) * TQ]
        a1 = res[1][1][c * TQ:(c + 1) * TQ]
        num = jnp.where(lower, a0, a1)
        den = pltpu.roll(jnp.where(lower, a1, a0), HEAD_DIM, 1)
        o_ref[:, c * LANES:(c + 1) * LANES] = (num / den).astype(BF16)


def _attn_call(q, qi, wi, kv, ki, batch, seq_len):
    n = q.shape[0]
    nq = seq_len // TQ
    topk = min(TOPK_MAX, seq_len // 4)
    qspec = lambda w: pl.BlockSpec((TQ, w), lambda b, i: (b * nq + i, 0))
    sspec = lambda w: pl.BlockSpec((seq_len, w), lambda b, i: (b, 0))
    wspec = pl.BlockSpec((WI_ROWS, TQ), lambda b, i: (0, b * nq + i))
    return pl.pallas_call(
        functools.partial(_attn_kernel, topk, seq_len),
        grid=(batch, nq),
        in_specs=[qspec(D_ATTN), qspec(IDX_HEADS * IDX_DIM), wspec, sspec(2 * LANES), sspec(LANES)],
        out_specs=qspec(D_ATTN),
        out_shape=jax.ShapeDtypeStruct((n, D_ATTN), BF16),
        scratch_shapes=[pltpu.VMEM((seq_len // KC, KC, TQ), jnp.int32),
                        pltpu.VMEM((seq_len // KC, TQ, KC), F32),
                        pltpu.VMEM((IDX_HEADS, TQ, LANES), BF16),
                        pltpu.VMEM((SUBLANES, TQ), jnp.int32),
                        pltpu.VMEM((seq_len // KC, KC, TQ), jnp.int16),
                        pltpu.VMEM((seq_len // KC, KC, TQ), jnp.int16)],
        compiler_params=_compiler_params(2),
        name="dsa_attn",
    )(q, qi, wi, kv, ki)


def _mix_kernel(alpha, h_ref, o_ref, ypre_ref, cpre_ref, wg_ref, bg_ref, wpo_ref, wco_ref, wao_ref,
                wo_ref, g_ref, b_ref, out_ref):
    h = h_ref[...]
    hb = h.astype(BF16)
    branches = ((ypre_ref, wpo_ref), (cpre_ref, wco_ref), (o_ref, wao_ref))
    merged = jnp.zeros((TM, D_MODEL), F32)
    for n, (x_ref, w_ref) in enumerate(branches):
        gate = jnp.dot(hb, wg_ref[:, n * D_MODEL:(n + 1) * D_MODEL], preferred_element_type=F32)
        gate = jax.nn.sigmoid(gate + bg_ref[:, n * D_MODEL:(n + 1) * D_MODEL])
        merged = merged + gate * jnp.dot(x_ref[...], w_ref[...], preferred_element_type=F32)
    mix = jnp.dot(merged.astype(BF16), wo_ref[...], preferred_element_type=F32)
    out_ref[...] = _layer_norm(alpha * h + mix, g_ref[...], b_ref[...])


def _mix_call(alpha, h, o, ypre, cpre, wg, bg, wpo, wco, wao, wo, g, b):
    n = h.shape[0]
    row_spec = lambda w: pl.BlockSpec((TM, w), lambda i: (i, 0))
    return pl.pallas_call(
        functools.partial(_mix_kernel, alpha),
        grid=(n // TM,),
        in_specs=[row_spec(D_MODEL), row_spec(D_ATTN), row_spec(D_POOL), row_spec(D_CONV)] + [_resident()] * 8,
        out_specs=row_spec(D_MODEL),
        out_shape=jax.ShapeDtypeStruct((n, D_MODEL), F32),
        compiler_params=_compiler_params(1),
        name="gated_merge",
    )(h, o, ypre, cpre, wg, bg, wpo, wco, wao, wo, g, b)


def _ffn_kernel(alpha, tiles_per_seq, h_ref, wu_ref, dw_ref, db_ref, wd_ref, g_ref, b_ref, out_ref,
                carry_ref, act_ref):
    i = pl.program_id(0)
    first = lax.rem(i, tiles_per_seq) == 0
    h = h_ref[...]
    hb = h.astype(BF16)

    @pl.when(first)
    def _():
        carry_ref[...] = jnp.zeros_like(carry_ref)

    def conv3(u, lo):
        cols = slice(lo, lo + FC)
        prev = carry_ref[:, cols]
        carry_ref[:, cols] = u[TM - FFN_HALO:, :]
        ext = jnp.concatenate([prev, u], axis=0)
        y = (dw_ref[2:3, cols] * u + dw_ref[1:2, cols] * pltpu.roll(ext, 1, 0)[FFN_HALO:]
             + dw_ref[0:1, cols] * pltpu.roll(ext, 2, 0)[FFN_HALO:])
        return y + db_ref[:, cols]

    def up(c):
        return tuple(jnp.dot(hb, wu_ref[:, lo:lo + FC], preferred_element_type=F32)
                     for lo in (c * FC, D_FF + c * FC))

    nc = D_FF // FC
    u_next = up(0)
    for c in range(nc):
        ug, uv = u_next
        if c + 1 < nc:
            u_next = up(c + 1)
        gate = conv3(ug, c * FC)
        val = conv3(uv, D_FF + c * FC)
        act_ref[:, c * FC:(c + 1) * FC] = (gate * jax.nn.sigmoid(gate) * val).astype(BF16)
    ffn = jnp.dot(act_ref[...], wd_ref[...], preferred_element_type=F32)
    out_ref[...] = _layer_norm(alpha * h + ffn, g_ref[...], b_ref[...])


def _ffn_call(alpha, h, wu, dw, db, wd, g, b, seq_len):
    n = h.shape[0]
    row_spec = pl.BlockSpec((TM, D_MODEL), lambda i: (i, 0))
    return pl.pallas_call(
        functools.partial(_ffn_kernel, alpha, seq_len // TM),
        grid=(n // TM,),
        in_specs=[row_spec] + [_resident()] * 6,
        out_specs=row_spec,
        out_shape=jax.ShapeDtypeStruct((n, D_MODEL), F32),
        scratch_shapes=[pltpu.VMEM((FFN_HALO, 2 * D_FF), F32), pltpu.VMEM((TM, D_FF), BF16)],
        compiler_params=_compiler_params(1),
        name="conv_ffn",
    )(h, wu, dw, db, wd, g, b)


def _head_pair_perm():
    group = N_HEADS // N_KV_HEADS
    cols = []
    for j in range(group):
        cols.extend(range(j * HEAD_DIM, (j + 1) * HEAD_DIM))
        cols.extend(range((group + j) * HEAD_DIM, (group + j + 1) * HEAD_DIM))
    return np.asarray(cols)


def _prep_in_weights(w_in, b_in):
    sizes = (D_POOL, 2 * D_CONV, D_ATTN, D_KV, D_KV, IDX_HEADS * IDX_DIM, IDX_DIM, IDX_HEADS, N_BRANCH * D_MODEL)
    offs = np.concatenate([[0], np.cumsum(sizes)])
    seg = lambda a, k: a[..., offs[k]:offs[k + 1]]
    perm = _head_pair_perm()

    def relayout(a):
        pad = jnp.zeros(a.shape[:-1] + (LANES - IDX_HEADS,), a.dtype)
        return jnp.concatenate([seg(a, 0), seg(a, 1), seg(a, 2)[..., perm], seg(a, 3), seg(a, 4), seg(a, 5),
                                seg(a, 6), seg(a, 6), seg(a, 7), pad], axis=-1)

    wa = relayout(w_in).astype(BF16)
    ba = relayout(b_in[None, :])
    wg = seg(w_in, 8).astype(BF16)
    bg = seg(b_in[None, :], 8)
    return wa, ba, wg, bg


def _block_diag(pool_w):
    out = jnp.zeros((D_POOL, D_POOL), pool_w.dtype)
    for g in range(POOL_GROUPS):
        lo = g * POOL_GROUP_DIM
        out = out.at[lo:lo + POOL_GROUP_DIM, lo:lo + POOL_GROUP_DIM].set(pool_w[g])
    return out


def kernel(x, positions, ln_in_g, ln_in_b, w_in, b_in, pool_w, pool_scale, w_pool_out, conv_dw_w, conv_dw_b,
           conv_ln_g, conv_ln_b, w_conv_out, w_attn_out, w_o, ln1_g, ln1_b, w_up, ffn_dw_w, ffn_dw_b, w_down,
           ln2_g, ln2_b):
    batch, seq_len, d_model = x.shape
    depth = w_in.shape[0]
    assert d_model == D_MODEL and seq_len % TM == 0 and seq_len % KC == 0 and KC % TQ == 0
    n = batch * seq_len
    alpha = float((2 * depth) ** 0.25)
    perm = _head_pair_perm()
    row = lambda a: a.reshape(1, -1)

    cos_t, sin_t = _rope_tables(positions)
    h = x.reshape(n, D_MODEL)
    for l in range(depth):
        wa, ba, wg, bg = _prep_in_weights(w_in[l], b_in[l])
        outs = _in_call(
            h, l == 0, row(ln_in_g), row(ln_in_b), cos_t, sin_t, wa, ba,
            _block_diag(pool_w[l]).astype(BF16), row(pool_scale[l]),
            conv_dw_w[l], row(conv_dw_b[l]), row(conv_ln_g[l]), row(conv_ln_b[l]), seq_len)
        q, kv, qi, ki, wi, ypre, cpre = outs[:7]
        if l == 0:
            h = outs[7]
        o = _attn_call(q, qi, wi, kv, ki, batch, seq_len)
        h = _mix_call(alpha, h, o, ypre, cpre, wg, bg, w_pool_out[l].astype(BF16), w_conv_out[l].astype(BF16),
                      w_attn_out[l][perm].astype(BF16), w_o[l].astype(BF16), row(ln1_g[l]), row(ln1_b[l]))
        h = _ffn_call(alpha, h, w_up[l].astype(BF16), ffn_dw_w[l], row(ffn_dw_b[l]), w_down[l].astype(BF16),
                      row(ln2_g[l]), row(ln2_b[l]), seq_len)
    return h.reshape(batch, seq_len, D_MODEL)
```

```python
import functools

import jax
import jax.numpy as jnp
import numpy as np
from jax import lax
from jax.experimental import pallas as pl
from jax.experimental.pallas import tpu as pltpu

F32 = jnp.float32
BF16 = jnp.bfloat16

D_MODEL = 1024
POOL_GROUPS = 4
POOL_GROUP_DIM = 64
POOL_WINDOWS = (2, 4, 8, 16)
D_POOL = POOL_GROUPS * POOL_GROUP_DIM
D_CONV = 256
CONV_WIDTH = 31
N_HEADS = 8
N_KV_HEADS = 2
HEAD_DIM = 64
D_ATTN = N_HEADS * HEAD_DIM
D_KV = N_KV_HEADS * HEAD_DIM
IDX_HEADS = 8
IDX_DIM = 64
TOPK_MAX = 256
ROPE_THETA = 10000.0
N_BRANCH = 3
D_FF = 2816
FFN_CONV_WIDTH = 3
LN_EPS = 1e-5

LANES = 128
SUBLANES = 8
VMEM_LIMIT_BYTES = 56 * 1024 * 1024

TM = 512
TQ = 256
COUNT_ROWS = 32
WI_ROWS = 8
KC = 512
SEL = 512
FLASH_SLOTS = 1
FC = 256
POOL_HALO = 16
CONV_HALO = 32
FFN_HALO = 8

SEG_POOL = (0, 256)
SEG_CONV = (256, 768)
SEG_Q = (768, 1280)
SEG_KV = (1280, 1536)
SEG_QI = (1536, 2048)
SEG_KI = (2048, 2176)
SEG_WI = (2176, 2304)
D_A = 2304

Q_SCALE = float(HEAD_DIM ** -0.5 * np.log2(np.e))
INT_MIN = -(2 ** 31)
HALF_BIAS = 2 ** 15
NEG_BIG = -1e30


def _layer_norm(x, g, b):
    mu = jnp.mean(x, axis=-1, keepdims=True)
    xc = x - mu
    var = jnp.mean(xc * xc, axis=-1, keepdims=True)
    return xc * lax.rsqrt(var + LN_EPS) * g + b


def _compiler_params(n_axes):
    return pltpu.CompilerParams(
        dimension_semantics=("arbitrary",) * n_axes,
        vmem_limit_bytes=VMEM_LIMIT_BYTES,
    )


def _resident():
    return pl.BlockSpec(memory_space=pltpu.VMEM)


def _rope_table_kernel(pos_ref, freq_ref, sign_ref, cos_ref, sin_ref):
    ang = pos_ref[...].astype(F32) * freq_ref[...]
    cos_ref[...] = jnp.cos(ang)
    sin_ref[...] = jnp.sin(ang) * sign_ref[...]


def _rope_tables(positions):
    n = positions.size
    half = HEAD_DIM // 2
    inv_freq = ROPE_THETA ** (-jnp.arange(half, dtype=F32) / half)
    freq_row = jnp.tile(inv_freq, LANES // half).reshape(1, LANES)
    lane = np.arange(LANES)
    sign_row = jnp.asarray(np.where(lane % HEAD_DIM < half, -1.0, 1.0), F32).reshape(1, LANES)
    rows = 1024
    return pl.pallas_call(
        _rope_table_kernel,
        grid=(n // rows,),
        in_specs=[pl.BlockSpec((rows, 1), lambda i: (i, 0)), _resident(), _resident()],
        out_specs=[pl.BlockSpec((rows, LANES), lambda i: (i, 0))] * 2,
        out_shape=[jax.ShapeDtypeStruct((n, LANES), F32)] * 2,
        compiler_params=_compiler_params(1),
        name="rope_tables",
    )(positions.reshape(n, 1), freq_row, sign_row)


def _rope128(x, cos, sin_signed, lower_half):
    partner = jnp.where(lower_half, pltpu.roll(x, LANES - HEAD_DIM // 2, 1), pltpu.roll(x, HEAD_DIM // 2, 1))
    return x * cos + partner * sin_signed


def _in_kernel(tiles_per_seq, entry_ln, h_ref, lng_ref, lnb_ref, cos_ref, sin_ref, wa_ref, ba_ref,
               poolw_ref, pscale_ref, dww_ref, dwb_ref, clng_ref, clnb_ref,
               q_ref, kv_ref, qi_ref, ki_ref, wi_ref, ypre_ref, cpre_ref, *rest):
    h_out_ref, pool_scr, conv_scr = rest if entry_ln else (None,) + rest
    i = pl.program_id(0)
    seq_tile = lax.rem(i, tiles_per_seq)
    first = seq_tile == 0
    h = h_ref[...]
    if entry_ln:
        h = _layer_norm(h, lng_ref[...], lnb_ref[...])
        h_out_ref[...] = h
    hb = h.astype(BF16)
    cos = cos_ref[...]
    sin = sin_ref[...]
    lane = lax.broadcasted_iota(jnp.int32, (TM, LANES), 1)
    lower_half = (lane & (HEAD_DIM - 1)) < (HEAD_DIM // 2)

    @pl.when(first)
    def _():
        pool_scr[...] = jnp.zeros_like(pool_scr)
        conv_scr[TM:TM + CONV_HALO, :] = jnp.zeros((CONV_HALO, D_CONV), F32)

    def proj(seg):
        lo, hi = seg
        return jnp.dot(hb, wa_ref[:, lo:hi], preferred_element_type=F32) + ba_ref[:, lo:hi]

    def rope_job(seg, out_ref, half, scale):
        def job():
            lo = seg[0] + half * 2 * LANES
            z = proj((lo, lo + 2 * LANES))
            for c in range(2):
                blk = _rope128(z[:, c * LANES:(c + 1) * LANES], cos, sin, lower_half)
                col = (2 * half + c) * LANES
                out_ref[:, col:col + LANES] = (blk * scale).astype(BF16)
        return job

    def kv_job():
        zkv = proj(SEG_KV)
        kv_ref[:, 0:LANES] = _rope128(zkv[:, 0:LANES], cos, sin, lower_half).astype(BF16)
        kv_ref[:, LANES:2 * LANES] = zkv[:, LANES:2 * LANES].astype(BF16)

    def index_key_job():
        z = proj((SEG_KI[0], SEG_WI[1]))
        ki_ref[...] = _rope128(z[:, 0:LANES], cos, sin, lower_half).astype(BF16)
        wi_t = jnp.transpose(z[:, LANES:2 * LANES] * (IDX_HEADS ** -0.5))
        wi_ref[...] = wi_t[0:WI_ROWS, :]

    def pool_job():
        u = proj(SEG_POOL)
        ext = jnp.concatenate([pool_scr[...], u], axis=0)
        pool_scr[...] = u[TM - POOL_HALO:, :]
        s2 = ext + pltpu.roll(ext, 1, 0)
        s4 = s2 + pltpu.roll(s2, 2, 0)
        s8 = s4 + pltpu.roll(s4, 4, 0)
        s16 = s8 + pltpu.roll(s8, 8, 0)
        row = lax.broadcasted_iota(jnp.int32, (TM, D_POOL), 0) + seq_tile * TM
        lane_p = lax.broadcasted_iota(jnp.int32, (TM, D_POOL), 1)
        grp = lane_p >> 6
        win = jnp.where(grp == 0, POOL_WINDOWS[0],
                        jnp.where(grp == 1, POOL_WINDOWS[1],
                                  jnp.where(grp == 2, POOL_WINDOWS[2], POOL_WINDOWS[3])))
        cnt = jnp.minimum(row + 1, win).astype(F32)
        wsum = jnp.where(grp == 0, s2[POOL_HALO:], jnp.where(grp == 1, s4[POOL_HALO:],
                         jnp.where(grp == 2, s8[POOL_HALO:], s16[POOL_HALO:])))
        mixed = wsum / cnt - u
        y = jnp.dot(mixed.astype(BF16), poolw_ref[...], preferred_element_type=F32)
        ypre_ref[...] = (y * pscale_ref[...]).astype(BF16)

    uc = proj(SEG_CONV)
    glu = uc[:, :D_CONV] * jax.nn.sigmoid(uc[:, D_CONV:])
    conv_scr[0:CONV_HALO, :] = conv_scr[TM:TM + CONV_HALO, :]
    conv_scr[CONV_HALO:CONV_HALO + TM, :] = glu
    acc = jnp.zeros((TM, D_CONV), F32) + dwb_ref[...]
    jobs = [rope_job(SEG_Q, q_ref, 0, Q_SCALE), rope_job(SEG_Q, q_ref, 1, Q_SCALE), kv_job,
            rope_job(SEG_QI, qi_ref, 0, IDX_DIM ** -0.5), rope_job(SEG_QI, qi_ref, 1, IDX_DIM ** -0.5),
            index_key_job, pool_job]
    ext_c = conv_scr[...]
    for b in range(SUBLANES):
        if b < len(jobs):
            jobs[b]()
        shifted = ext_c if b == 0 else pltpu.roll(ext_c, b, 0)
        for a in range(CONV_HALO // SUBLANES):
            d = SUBLANES * a + b
            if d < CONV_WIDTH:
                lo = CONV_HALO - SUBLANES * a
                acc = acc + dww_ref[CONV_WIDTH - 1 - d:CONV_WIDTH - d, :] * shifted[lo:lo + TM, :]
    assert len(jobs) <= SUBLANES
    hc = _layer_norm(acc, clng_ref[...], clnb_ref[...])
    cpre_ref[...] = (hc * jax.nn.sigmoid(hc)).astype(BF16)


def _in_call(h, entry_ln, lng, lnb, cos_t, sin_t, wa, ba, poolw, pscale, dww, dwb, clng, clnb, seq_len):
    n = h.shape[0]
    tiles_per_seq = seq_len // TM
    row_spec = lambda w: pl.BlockSpec((TM, w), lambda i: (i, 0))
    out_widths = (D_ATTN, 2 * LANES, IDX_HEADS * IDX_DIM, LANES, D_POOL, D_CONV)
    out_dtypes = (BF16, BF16, BF16, BF16, BF16, BF16)
    out_specs = [row_spec(w) for w in out_widths]
    out_shape = [jax.ShapeDtypeStruct((n, w), d) for w, d in zip(out_widths, out_dtypes)]
    out_specs.insert(4, pl.BlockSpec((WI_ROWS, TM), lambda i: (0, i)))
    out_shape.insert(4, jax.ShapeDtypeStruct((WI_ROWS, n), F32))
    if entry_ln:
        out_specs.append(row_spec(D_MODEL))
        out_shape.append(jax.ShapeDtypeStruct((n, D_MODEL), F32))
    return pl.pallas_call(
        functools.partial(_in_kernel, tiles_per_seq, entry_ln),
        grid=(n // TM,),
        in_specs=([row_spec(D_MODEL), _resident(), _resident(), row_spec(LANES), row_spec(LANES)]
                  + [_resident()] * 8),
        out_specs=out_specs,
        out_shape=out_shape,
        scratch_shapes=[pltpu.VMEM((POOL_HALO, D_POOL), F32),
                        pltpu.VMEM((TM + CONV_HALO, D_CONV), F32)],
        compiler_params=_compiler_params(1),
        name="in_proj",
    )(h, lng, lnb, cos_t, sin_t, wa, ba, poolw, pscale, dww, dwb, clng, clnb)


def _attn_kernel(topk, seq_len, q_ref, qi_ref, wit_ref, kv_ref, ki_ref, o_ref,
                 key_scr, bias_scr, qi_scr, jmax_scr, hi_scr, lo_scr, m_scr, acc_scr):
    i = pl.program_id(1)
    n_chunks = (i * TQ + TQ - 1) // KC + 1
    lane = lax.broadcasted_iota(jnp.int32, (TQ, LANES), 1)
    lower = lane < HEAD_DIM
    nt_dims = (((1,), (1,)), ((), ()))
    zero_b = jnp.zeros((TQ, LANES), BF16)
    kpos_t = lax.broadcasted_iota(jnp.int32, (KC, TQ), 0)
    qpos_t = i * TQ + lax.broadcasted_iota(jnp.int32, (KC, TQ), 1)
    qpos_row = i * TQ + lax.broadcasted_iota(jnp.int32, (1, TQ), 1)

    qi = qi_ref[...]
    for h in range(IDX_HEADS):
        blk = qi[:, (h // 2) * LANES:(h // 2 + 1) * LANES]
        qi_scr[h] = jnp.where(lower if h % 2 == 0 else jnp.logical_not(lower), blk, zero_b)
    wit = wit_ref[...]

    def score_chunk(kc, carry):
        k0 = pl.multiple_of(kc * KC, KC)
        kic = ki_ref[pl.ds(k0, KC), :]
        acc = jnp.zeros((KC, TQ), F32)
        for h in range(IDX_HEADS):
            logits = lax.dot_general(kic, qi_scr[h], nt_dims, preferred_element_type=F32)
            acc = acc + jnp.maximum(logits, 0.0) * wit[h:h + 1, :]
        bits = lax.bitcast_convert_type(acc, jnp.int32)
        key = bits ^ ((bits >> 31) & 0x7FFFFFFF)
        key = jnp.where(kpos_t + k0 <= qpos_t, key, INT_MIN)
        hi = (key >> 16).astype(jnp.int16)
        lo = ((key & 0xFFFF) - HALF_BIAS).astype(jnp.int16)
        for part in range(KC // SEL):
            rows_p = slice(part * SEL, (part + 1) * SEL)
            slot = kc * (KC // SEL) + part
            key_scr[slot] = key[rows_p]
            hi_scr[slot] = hi[rows_p]
            lo_scr[slot] = lo[rows_p]
        return carry

    lax.fori_loop(0, n_chunks, score_chunk, 0)

    n_sel = (i * TQ + TQ - 1) // SEL + 1
    kpos_s = lax.broadcasted_iota(jnp.int32, (SEL, TQ), 0)

    def count(pred):
        def body(slot, acc):
            hit = pred(key_scr[slot], slot)
            ones = jnp.where(hit, 1.0, 0.0)
            return acc + jnp.sum(ones.reshape(SEL // COUNT_ROWS, COUNT_ROWS, TQ), axis=0)
        acc = lax.fori_loop(0, n_sel, body, jnp.zeros((COUNT_ROWS, TQ), F32))
        return jnp.sum(acc, axis=0, keepdims=True)

    def count16(src_scr, pred):
        def body(slot, acc):
            ones = jnp.where(pred(src_scr[slot]), jnp.int16(1), jnp.int16(0))
            for r0 in range(0, SEL, COUNT_ROWS):
                acc = acc + ones[r0:r0 + COUNT_ROWS]
            return acc
        acc = lax.fori_loop(0, n_sel, body, jnp.zeros((COUNT_ROWS, TQ), jnp.int16))
        return jnp.sum(acc.astype(F32), axis=0, keepdims=True)

    kf = float(topk)

    def search16(src_scr, base):
        c0 = count16(src_scr, lambda x: x >= 0)
        t0 = jnp.where(base + c0 >= kf, 0, -HALF_BIAS).astype(jnp.int32)

        def step(it, t):
            cand = t + (jnp.int32(1) << (14 - it))
            cand16 = cand.astype(jnp.int16)
            c = count16(src_scr, lambda x: x >= cand16)
            return jnp.where(base + c >= kf, cand, t)

        return lax.fori_loop(0, 15, step, t0)

    thr_hi = search16(hi_scr, 0.0)
    thr_hi16 = thr_hi.astype(jnp.int16)

    def mask_lo(slot, carry):
        lo_scr[slot] = jnp.where(hi_scr[slot] == thr_hi16, lo_scr[slot], jnp.int16(-HALF_BIAS))
        return carry

    lax.fori_loop(0, n_sel, mask_lo, 0)
    thr_lo = search16(lo_scr, count16(hi_scr, lambda x: x > thr_hi16))
    thr = thr_hi * (2 * HALF_BIAS) + (thr_lo + HALF_BIAS)

    cnt_ge = count(lambda key, slot: key >= thr)
    need = kf - count(lambda key, slot: key > thr)
    jmax_scr[...] = jnp.full((SUBLANES, TQ), seq_len, jnp.int32)

    @pl.when(jnp.max(jnp.where(thr == INT_MIN, 0.0, cnt_ge)) > kf)
    def _():
        tie_bits = (seq_len - 1).bit_length()

        def tie_step(it, jmax):
            cand = jmax + (jnp.int32(1) << (tie_bits - 1 - it))
            c = count(lambda key, slot: (key == thr) & (kpos_s + slot * SEL < cand))
            return jnp.where(c < need, cand, jmax)

        jmax = lax.fori_loop(0, tie_bits, tie_step, jnp.zeros((1, TQ), jnp.int32))
        jmax_scr[...] = jnp.broadcast_to(jmax, (SUBLANES, TQ))

    jmax = jnp.where(thr == INT_MIN, qpos_row, jmax_scr[0:1, :])

    def bias_slot(slot, carry):
        key = key_scr[slot]
        kp = kpos_s + slot * SEL
        sel = (key > thr) | ((key == thr) & (kp <= jmax))
        bias_scr[slot] = jnp.transpose(jnp.where(sel, 0.0, NEG_BIG))
        return carry

    lax.fori_loop(0, n_chunks * (KC // SEL), bias_slot, 0)

    qv = q_ref[...]
    rows = (N_HEADS // N_KV_HEADS) * TQ
    qs = []
    for n in range(N_KV_HEADS):
        keep = lower if n == 0 else jnp.logical_not(lower)
        parts = [jnp.where(keep, qv[:, c * LANES:(c + 1) * LANES], zero_b) for c in range(D_ATTN // LANES)]
        qs.append(jnp.concatenate(parts, axis=0))

    def flash_update(first_slot, n_slots):
        width = n_slots * SEL
        k0 = pl.multiple_of(first_slot * SEL, SEL)
        kvc = kv_ref[pl.ds(k0, width), :]
        kch = kvc[:, 0:LANES]
        vch = kvc[:, LANES:2 * LANES]
        bias = jnp.concatenate([bias_scr[first_slot + part] for part in range(n_slots)], axis=1)
        bias4 = jnp.concatenate([bias] * (N_HEADS // N_KV_HEADS), axis=0)
        lane_k = lax.broadcasted_iota(jnp.int32, (width, LANES), 1)
        for n in range(N_KV_HEADS):
            m = m_scr[n]
            own = (lane_k < HEAD_DIM) if n == 0 else (lane_k >= HEAD_DIM)
            v_aug = jnp.where(own, vch, jnp.ones_like(vch))
            s = lax.dot_general(qs[n], kch, nt_dims, preferred_element_type=F32) + bias4
            row_max = jnp.broadcast_to(jnp.max(s, axis=-1, keepdims=True), (rows, LANES))
            m_new = jnp.maximum(m, row_max)
            p = jnp.exp2((s - jnp.concatenate([m_new] * (width // LANES), axis=1)).astype(BF16))
            acc_scr[n] = jnp.exp2(m - m_new) * acc_scr[n] + jnp.dot(p, v_aug, preferred_element_type=F32)
            m_scr[n] = m_new

    m_scr[...] = jnp.full(m_scr.shape, NEG_BIG, F32)
    acc_scr[...] = jnp.zeros(acc_scr.shape, F32)
    n_slots_total = n_chunks * (KC // SEL)

    def flash_wide(j, carry):
        flash_update(j * FLASH_SLOTS, FLASH_SLOTS)
        return carry

    lax.fori_loop(0, n_slots_total // FLASH_SLOTS, flash_wide, 0)
    for rem in range(1, FLASH_SLOTS):
        @pl.when(n_slots_total % FLASH_SLOTS == rem)
        def _(rem=rem):
            flash_update(n_slots_total - rem, rem)

    for c in range(D_ATTN // LANES):
        a0 = acc_scr[0, c * TQ:(c + 1) * TQ, :]
        a1 = acc_scr[1, c * TQ:(c + 1) * TQ, :]
        num = jnp.where(lower, a0, a1)
        den = pltpu.roll(jnp.where(lower, a1, a0), HEAD_DIM, 1)
        o_ref[:, c * LANES:(c + 1) * LANES] = (num / den).astype(BF16)


def _attn_call(q, qi, wi, kv, ki, batch, seq_len):
    n = q.shape[0]
    nq = seq_len // TQ
    topk = min(TOPK_MAX, seq_len // 4)
    qspec = lambda w: pl.BlockSpec((TQ, w), lambda b, i: (b * nq + i, 0))
    sspec = lambda w: pl.BlockSpec((seq_len, w), lambda b, i: (b, 0))
    wspec = pl.BlockSpec((WI_ROWS, TQ), lambda b, i: (0, b * nq + i))
    return pl.pallas_call(
        functools.partial(_attn_kernel, topk, seq_len),
        grid=(batch, nq),
        in_specs=[qspec(D_ATTN), qspec(IDX_HEADS * IDX_DIM), wspec, sspec(2 * LANES), sspec(LANES)],
        out_specs=qspec(D_ATTN),
        out_shape=jax.ShapeDtypeStruct((n, D_ATTN), BF16),
        scratch_shapes=[pltpu.VMEM((seq_len // SEL, SEL, TQ), jnp.int32),
                        pltpu.VMEM((seq_len // SEL, TQ, SEL), F32),
                        pltpu.VMEM((IDX_HEADS, TQ, LANES), BF16),
                        pltpu.VMEM((SUBLANES, TQ), jnp.int32),
                        pltpu.VMEM((seq_len // SEL, SEL, TQ), jnp.int16),
                        pltpu.VMEM((seq_len // SEL, SEL, TQ), jnp.int16),
                        pltpu.VMEM((N_KV_HEADS, (N_HEADS // N_KV_HEADS) * TQ, LANES), F32),
                        pltpu.VMEM((N_KV_HEADS, (N_HEADS // N_KV_HEADS) * TQ, LANES), F32)],
        compiler_params=_compiler_params(2),
        name="dsa_attn",
    )(q, qi, wi, kv, ki)


def _mix_kernel(alpha, h_ref, o_ref, ypre_ref, cpre_ref, wg_ref, bg_ref, wpo_ref, wco_ref, wao_ref,
                wo_ref, g_ref, b_ref, out_ref):
    h = h_ref[...]
    hb = h.astype(BF16)
    branches = ((ypre_ref, wpo_ref), (cpre_ref, wco_ref), (o_ref, wao_ref))
    merged = jnp.zeros((TM, D_MODEL), F32)
    for n, (x_ref, w_ref) in enumerate(branches):
        gate = jnp.dot(hb, wg_ref[:, n * D_MODEL:(n + 1) * D_MODEL], preferred_element_type=F32)
        gate = jax.nn.sigmoid(gate + bg_ref[:, n * D_MODEL:(n + 1) * D_MODEL])
        merged = merged + gate * jnp.dot(x_ref[...], w_ref[...], preferred_element_type=F32)
    mix = jnp.dot(merged.astype(BF16), wo_ref[...], preferred_element_type=F32)
    out_ref[...] = _layer_norm(alpha * h + mix, g_ref[...], b_ref[...])


def _mix_call(alpha, h, o, ypre, cpre, wg, bg, wpo, wco, wao, wo, g, b):
    n = h.shape[0]
    row_spec = lambda w: pl.BlockSpec((TM, w), lambda i: (i, 0))
    return pl.pallas_call(
        functools.partial(_mix_kernel, alpha),
        grid=(n // TM,),
        in_specs=[row_spec(D_MODEL), row_spec(D_ATTN), row_spec(D_POOL), row_spec(D_CONV)] + [_resident()] * 8,
        out_specs=row_spec(D_MODEL),
        out_shape=jax.ShapeDtypeStruct((n, D_MODEL), F32),
        compiler_params=_compiler_params(1),
        name="gated_merge",
    )(h, o, ypre, cpre, wg, bg, wpo, wco, wao, wo, g, b)


def _ffn_kernel(alpha, tiles_per_seq, h_ref, wu_ref, dw_ref, db_ref, wd_ref, g_ref, b_ref, out_ref,
                carry_ref, act_ref):
    i = pl.program_id(0)
    first = lax.rem(i, tiles_per_seq) == 0
    h = h_ref[...]
    hb = h.astype(BF16)

    @pl.when(first)
    def _():
        carry_ref[...] = jnp.zeros_like(carry_ref)

    def conv3(u, lo):
        cols = slice(lo, lo + FC)
        prev = carry_ref[:, cols]
        carry_ref[:, cols] = u[TM - FFN_HALO:, :]
        ext = jnp.concatenate([prev, u], axis=0)
        y = (dw_ref[2:3, cols] * u + dw_ref[1:2, cols] * pltpu.roll(ext, 1, 0)[FFN_HALO:]
             + dw_ref[0:1, cols] * pltpu.roll(ext, 2, 0)[FFN_HALO:])
        return y + db_ref[:, cols]

    def up(c):
        return tuple(jnp.dot(hb, wu_ref[:, lo:lo + FC], preferred_element_type=F32)
                     for lo in (c * FC, D_FF + c * FC))

    nc = D_FF // FC
    u_next = up(0)
    for c in range(nc):
        ug, uv = u_next
        if c + 1 < nc:
            u_next = up(c + 1)
        gate = conv3(ug, c * FC)
        val = conv3(uv, D_FF + c * FC)
        act_ref[:, c * FC:(c + 1) * FC] = (gate * jax.nn.sigmoid(gate) * val).astype(BF16)
    ffn = jnp.dot(act_ref[...], wd_ref[...], preferred_element_type=F32)
    out_ref[...] = _layer_norm(alpha * h + ffn, g_ref[...], b_ref[...])


def _ffn_call(alpha, h, wu, dw, db, wd, g, b, seq_len):
    n = h.shape[0]
    row_spec = pl.BlockSpec((TM, D_MODEL), lambda i: (i, 0))
    return pl.pallas_call(
        functools.partial(_ffn_kernel, alpha, seq_len // TM),
        grid=(n // TM,),
        in_specs=[row_spec] + [_resident()] * 6,
        out_specs=row_spec,
        out_shape=jax.ShapeDtypeStruct((n, D_MODEL), F32),
        scratch_shapes=[pltpu.VMEM((FFN_HALO, 2 * D_FF), F32), pltpu.VMEM((TM, D_FF), BF16)],
        compiler_params=_compiler_params(1),
        name="conv_ffn",
    )(h, wu, dw, db, wd, g, b)


def _head_pair_perm():
    group = N_HEADS // N_KV_HEADS
    cols = []
    for j in range(group):
        cols.extend(range(j * HEAD_DIM, (j + 1) * HEAD_DIM))
        cols.extend(range((group + j) * HEAD_DIM, (group + j + 1) * HEAD_DIM))
    return np.asarray(cols)


def _prep_in_weights(w_in, b_in):
    sizes = (D_POOL, 2 * D_CONV, D_ATTN, D_KV, D_KV, IDX_HEADS * IDX_DIM, IDX_DIM, IDX_HEADS, N_BRANCH * D_MODEL)
    offs = np.concatenate([[0], np.cumsum(sizes)])
    seg = lambda a, k: a[..., offs[k]:offs[k + 1]]
    perm = _head_pair_perm()

    def relayout(a):
        pad = jnp.zeros(a.shape[:-1] + (LANES - IDX_HEADS,), a.dtype)
        return jnp.concatenate([seg(a, 0), seg(a, 1), seg(a, 2)[..., perm], seg(a, 3), seg(a, 4), seg(a, 5),
                                seg(a, 6), seg(a, 6), seg(a, 7), pad], axis=-1)

    wa = relayout(w_in).astype(BF16)
    ba = relayout(b_in[None, :])
    wg = seg(w_in, 8).astype(BF16)
    bg = seg(b_in[None, :], 8)
    return wa, ba, wg, bg


def _block_diag(pool_w):
    out = jnp.zeros((D_POOL, D_POOL), pool_w.dtype)
    for g in range(POOL_GROUPS):
        lo = g * POOL_GROUP_DIM
        out = out.at[lo:lo + POOL_GROUP_DIM, lo:lo + POOL_GROUP_DIM].set(pool_w[g])
    return out


def kernel(x, positions, ln_in_g, ln_in_b, w_in, b_in, pool_w, pool_scale, w_pool_out, conv_dw_w, conv_dw_b,
           conv_ln_g, conv_ln_b, w_conv_out, w_attn_out, w_o, ln1_g, ln1_b, w_up, ffn_dw_w, ffn_dw_b, w_down,
           ln2_g, ln2_b):
    batch, seq_len, d_model = x.shape
    depth = w_in.shape[0]
    assert d_model == D_MODEL and seq_len % TM == 0 and seq_len % KC == 0 and KC % TQ == 0
    assert KC % SEL == 0 and SEL % COUNT_ROWS == 0
    n = batch * seq_len
    alpha = float((2 * depth) ** 0.25)
    perm = _head_pair_perm()
    row = lambda a: a.reshape(1, -1)

    cos_t, sin_t = _rope_tables(positions)
    h = x.reshape(n, D_MODEL)
    for l in range(depth):
        wa, ba, wg, bg = _prep_in_weights(w_in[l], b_in[l])
        outs = _in_call(
            h, l == 0, row(ln_in_g), row(ln_in_b), cos_t, sin_t, wa, ba,
            _block_diag(pool_w[l]).astype(BF16), row(pool_scale[l]),
            conv_dw_w[l], row(conv_dw_b[l]), row(conv_ln_g[l]), row(conv_ln_b[l]), seq_len)
        q, kv, qi, ki, wi, ypre, cpre = outs[:7]
        if l == 0:
            h = outs[7]
        o = _attn_call(q, qi, wi, kv, ki, batch, seq_len)
        h = _mix_call(alpha, h, o, ypre, cpre, wg, bg, w_pool_out[l].astype(BF16), w_conv_out[l].astype(BF16),
                      w_attn_out[l][perm].astype(BF16), w_o[l].astype(BF16), row(ln1_g[l]), row(ln1_b[l]))
        h = _ffn_call(alpha, h, w_up[l].astype(BF16), ffn_dw_w[l], row(ffn_dw_b[l]), w_down[l].astype(BF16),
                      row(ln2_g[l]), row(ln2_b[l]), seq_len)
    return h.reshape(batch, seq_len, D_MODEL)
```

```python
import functools

import jax
import jax.numpy as jnp
import numpy as np
from jax import lax
from jax.experimental import pallas as pl
from jax.experimental.pallas import tpu as pltpu

F32 = jnp.float32
BF16 = jnp.bfloat16

D_MODEL = 1024
POOL_GROUPS = 4
POOL_GROUP_DIM = 64
POOL_WINDOWS = (2, 4, 8, 16)
D_POOL = POOL_GROUPS * POOL_GROUP_DIM
D_CONV = 256
CONV_WIDTH = 31
N_HEADS = 8
N_KV_HEADS = 2
HEAD_DIM = 64
D_ATTN = N_HEADS * HEAD_DIM
D_KV = N_KV_HEADS * HEAD_DIM
IDX_HEADS = 8
IDX_DIM = 64
TOPK_MAX = 256
ROPE_THETA = 10000.0
N_BRANCH = 3
D_FF = 2816
FFN_CONV_WIDTH = 3
LN_EPS = 1e-5

LANES = 128
SUBLANES = 8
VMEM_LIMIT_BYTES = 56 * 1024 * 1024

TM = 512
TQ = 256
COUNT_ROWS = 32
WI_ROWS = 8
KC = 512
SEL = 512
FLASH_SLOTS = 1
FC = 256
POOL_HALO = 16
CONV_HALO = 32
FFN_HALO = 8

SEG_POOL = (0, 256)
SEG_CONV = (256, 768)
SEG_Q = (768, 1280)
SEG_KV = (1280, 1536)
SEG_QI = (1536, 2048)
SEG_KI = (2048, 2176)
SEG_WI = (2176, 2304)
D_A = 2304

Q_SCALE = float(HEAD_DIM ** -0.5 * np.log2(np.e))
INT_MIN = -(2 ** 31)
HALF_BIAS = 2 ** 15
NEG_BIG = -1e30


def _layer_norm(x, g, b):
    mu = jnp.mean(x, axis=-1, keepdims=True)
    xc = x - mu
    var = jnp.mean(xc * xc, axis=-1, keepdims=True)
    return xc * lax.rsqrt(var + LN_EPS) * g + b


def _compiler_params(n_axes):
    return pltpu.CompilerParams(
        dimension_semantics=("arbitrary",) * n_axes,
        vmem_limit_bytes=VMEM_LIMIT_BYTES,
    )


def _resident():
    return pl.BlockSpec(memory_space=pltpu.VMEM)


def _rope_table_kernel(pos_ref, freq_ref, sign_ref, cos_ref, sin_ref):
    ang = pos_ref[...].astype(F32) * freq_ref[...]
    cos_ref[...] = jnp.cos(ang)
    sin_ref[...] = jnp.sin(ang) * sign_ref[...]


def _rope_tables(positions):
    n = positions.size
    half = HEAD_DIM // 2
    inv_freq = ROPE_THETA ** (-jnp.arange(half, dtype=F32) / half)
    freq_row = jnp.tile(inv_freq, LANES // half).reshape(1, LANES)
    lane = np.arange(LANES)
    sign_row = jnp.asarray(np.where(lane % HEAD_DIM < half, -1.0, 1.0), F32).reshape(1, LANES)
    rows = 1024
    return pl.pallas_call(
        _rope_table_kernel,
        grid=(n // rows,),
        in_specs=[pl.BlockSpec((rows, 1), lambda i: (i, 0)), _resident(), _resident()],
        out_specs=[pl.BlockSpec((rows, LANES), lambda i: (i, 0))] * 2,
        out_shape=[jax.ShapeDtypeStruct((n, LANES), F32)] * 2,
        compiler_params=_compiler_params(1),
        name="rope_tables",
    )(positions.reshape(n, 1), freq_row, sign_row)


def _rope128(x, cos, sin_signed, lower_half):
    partner = jnp.where(lower_half, pltpu.roll(x, LANES - HEAD_DIM // 2, 1), pltpu.roll(x, HEAD_DIM // 2, 1))
    return x * cos + partner * sin_signed


def _in_kernel(tiles_per_seq, entry_ln, h_ref, lng_ref, lnb_ref, cos_ref, sin_ref, wa_ref, ba_ref,
               poolw_ref, pscale_ref, dww_ref, dwb_ref, clng_ref, clnb_ref,
               q_ref, kv_ref, qi_ref, ki_ref, wi_ref, ypre_ref, cpre_ref, *rest):
    h_out_ref, pool_scr, conv_scr = rest if entry_ln else (None,) + rest
    i = pl.program_id(0)
    seq_tile = lax.rem(i, tiles_per_seq)
    first = seq_tile == 0
    h = h_ref[...]
    if entry_ln:
        h = _layer_norm(h, lng_ref[...], lnb_ref[...])
        h_out_ref[...] = h
    hb = h.astype(BF16)
    cos = cos_ref[...]
    sin = sin_ref[...]
    lane = lax.broadcasted_iota(jnp.int32, (TM, LANES), 1)
    lower_half = (lane & (HEAD_DIM - 1)) < (HEAD_DIM // 2)

    @pl.when(first)
    def _():
        pool_scr[...] = jnp.zeros_like(pool_scr)
        conv_scr[TM:TM + CONV_HALO, :] = jnp.zeros((CONV_HALO, D_CONV), F32)

    def proj(seg):
        lo, hi = seg
        return jnp.dot(hb, wa_ref[:, lo:hi], preferred_element_type=F32) + ba_ref[:, lo:hi]

    def rope_job(seg, out_ref, half, scale):
        def job():
            lo = seg[0] + half * 2 * LANES
            z = proj((lo, lo + 2 * LANES))
            for c in range(2):
                blk = _rope128(z[:, c * LANES:(c + 1) * LANES], cos, sin, lower_half)
                col = (2 * half + c) * LANES
                out_ref[:, col:col + LANES] = (blk * scale).astype(BF16)
        return job

    def kv_job():
        zkv = proj(SEG_KV)
        kv_ref[:, 0:LANES] = _rope128(zkv[:, 0:LANES], cos, sin, lower_half).astype(BF16)
        kv_ref[:, LANES:2 * LANES] = zkv[:, LANES:2 * LANES].astype(BF16)

    def index_key_job():
        z = proj((SEG_KI[0], SEG_WI[1]))
        ki_ref[...] = _rope128(z[:, 0:LANES], cos, sin, lower_half).astype(BF16)
        wi_t = jnp.transpose(z[:, LANES:2 * LANES] * (IDX_HEADS ** -0.5))
        wi_ref[...] = wi_t[0:WI_ROWS, :]

    def pool_job():
        u = proj(SEG_POOL)
        ext = jnp.concatenate([pool_scr[...], u], axis=0)
        pool_scr[...] = u[TM - POOL_HALO:, :]
        s2 = ext + pltpu.roll(ext, 1, 0)
        s4 = s2 + pltpu.roll(s2, 2, 0)
        s8 = s4 + pltpu.roll(s4, 4, 0)
        s16 = s8 + pltpu.roll(s8, 8, 0)
        row = lax.broadcasted_iota(jnp.int32, (TM, D_POOL), 0) + seq_tile * TM
        lane_p = lax.broadcasted_iota(jnp.int32, (TM, D_POOL), 1)
        grp = lane_p >> 6
        win = jnp.where(grp == 0, POOL_WINDOWS[0],
                        jnp.where(grp == 1, POOL_WINDOWS[1],
                                  jnp.where(grp == 2, POOL_WINDOWS[2], POOL_WINDOWS[3])))
        cnt = jnp.minimum(row + 1, win).astype(F32)
        wsum = jnp.where(grp == 0, s2[POOL_HALO:], jnp.where(grp == 1, s4[POOL_HALO:],
                         jnp.where(grp == 2, s8[POOL_HALO:], s16[POOL_HALO:])))
        mixed = wsum / cnt - u
        y = jnp.dot(mixed.astype(BF16), poolw_ref[...], preferred_element_type=F32)
        ypre_ref[...] = (y * pscale_ref[...]).astype(BF16)

    uc = proj(SEG_CONV)
    glu = uc[:, :D_CONV] * jax.nn.sigmoid(uc[:, D_CONV:])
    conv_scr[0:CONV_HALO, :] = conv_scr[TM:TM + CONV_HALO, :]
    conv_scr[CONV_HALO:CONV_HALO + TM, :] = glu
    acc = jnp.zeros((TM, D_CONV), F32) + dwb_ref[...]
    jobs = [rope_job(SEG_Q, q_ref, 0, Q_SCALE), rope_job(SEG_Q, q_ref, 1, Q_SCALE), kv_job,
            rope_job(SEG_QI, qi_ref, 0, IDX_DIM ** -0.5), rope_job(SEG_QI, qi_ref, 1, IDX_DIM ** -0.5),
            index_key_job, pool_job]
    ext_c = conv_scr[...]
    for b in range(SUBLANES):
        if b < len(jobs):
            jobs[b]()
        shifted = ext_c if b == 0 else pltpu.roll(ext_c, b, 0)
        for a in range(CONV_HALO // SUBLANES):
            d = SUBLANES * a + b
            if d < CONV_WIDTH:
                lo = CONV_HALO - SUBLANES * a
                acc = acc + dww_ref[CONV_WIDTH - 1 - d:CONV_WIDTH - d, :] * shifted[lo:lo + TM, :]
    assert len(jobs) <= SUBLANES
    hc = _layer_norm(acc, clng_ref[...], clnb_ref[...])
    cpre_ref[...] = (hc * jax.nn.sigmoid(hc)).astype(BF16)


def _in_call(h, entry_ln, lng, lnb, cos_t, sin_t, wa, ba, poolw, pscale, dww, dwb, clng, clnb, seq_len):
    n = h.shape[0]
    tiles_per_seq = seq_len // TM
    row_spec = lambda w: pl.BlockSpec((TM, w), lambda i: (i, 0))
    out_widths = (D_ATTN, 2 * LANES, IDX_HEADS * IDX_DIM, LANES, D_POOL, D_CONV)
    out_dtypes = (BF16, BF16, BF16, BF16, BF16, BF16)
    out_specs = [row_spec(w) for w in out_widths]
    out_shape = [jax.ShapeDtypeStruct((n, w), d) for w, d in zip(out_widths, out_dtypes)]
    out_specs.insert(4, pl.BlockSpec((WI_ROWS, TM), lambda i: (0, i)))
    out_shape.insert(4, jax.ShapeDtypeStruct((WI_ROWS, n), F32))
    if entry_ln:
        out_specs.append(row_spec(D_MODEL))
        out_shape.append(jax.ShapeDtypeStruct((n, D_MODEL), F32))
    return pl.pallas_call(
        functools.partial(_in_kernel, tiles_per_seq, entry_ln),
        grid=(n // TM,),
        in_specs=([row_spec(D_MODEL), _resident(), _resident(), row_spec(LANES), row_spec(LANES)]
                  + [_resident()] * 8),
        out_specs=out_specs,
        out_shape=out_shape,
        scratch_shapes=[pltpu.VMEM((POOL_HALO, D_POOL), F32),
                        pltpu.VMEM((TM + CONV_HALO, D_CONV), F32)],
        compiler_params=_compiler_params(1),
        name="in_proj",
    )(h, lng, lnb, cos_t, sin_t, wa, ba, poolw, pscale, dww, dwb, clng, clnb)


def _attn_kernel(topk, seq_len, q_ref, qi_ref, wit_ref, kv_ref, ki_ref, o_ref,
                 key_scr, bias_scr, qi_scr, jmax_scr, hi_scr, lo_scr, m_scr, acc_scr):
    i = pl.program_id(1)
    n_chunks = (i * TQ + TQ - 1) // KC + 1
    lane = lax.broadcasted_iota(jnp.int32, (TQ, LANES), 1)
    lower = lane < HEAD_DIM
    nt_dims = (((1,), (1,)), ((), ()))
    zero_b = jnp.zeros((TQ, LANES), BF16)
    kpos_t = lax.broadcasted_iota(jnp.int32, (KC, TQ), 0)
    qpos_t = i * TQ + lax.broadcasted_iota(jnp.int32, (KC, TQ), 1)
    qpos_row = i * TQ + lax.broadcasted_iota(jnp.int32, (1, TQ), 1)

    qi = qi_ref[...]
    for h in range(IDX_HEADS):
        blk = qi[:, (h // 2) * LANES:(h // 2 + 1) * LANES]
        qi_scr[h] = jnp.where(lower if h % 2 == 0 else jnp.logical_not(lower), blk, zero_b)
    wit = wit_ref[...]

    def score_chunk(kc, carry):
        k0 = pl.multiple_of(kc * KC, KC)
        kic = ki_ref[pl.ds(k0, KC), :]
        acc = jnp.zeros((KC, TQ), F32)
        for h in range(IDX_HEADS):
            logits = lax.dot_general(kic, qi_scr[h], nt_dims, preferred_element_type=F32)
            acc = acc + jnp.maximum(logits, 0.0) * wit[h:h + 1, :]
        bits = lax.bitcast_convert_type(acc, jnp.int32)
        key = bits ^ ((bits >> 31) & 0x7FFFFFFF)
        key = jnp.where(kpos_t + k0 <= qpos_t, key, INT_MIN)
        hi = (key >> 16).astype(jnp.int16)
        lo = ((key & 0xFFFF) - HALF_BIAS).astype(jnp.int16)
        for part in range(KC // SEL):
            rows_p = slice(part * SEL, (part + 1) * SEL)
            slot = kc * (KC // SEL) + part
            key_scr[slot] = key[rows_p]
            hi_scr[slot] = hi[rows_p]
            lo_scr[slot] = lo[rows_p]
        return carry

    lax.fori_loop(0, n_chunks, score_chunk, 0)

    n_sel = (i * TQ + TQ - 1) // SEL + 1
    kpos_s = lax.broadcasted_iota(jnp.int32, (SEL, TQ), 0)

    def count(pred):
        def body(slot, acc):
            hit = pred(key_scr[slot], slot)
            ones = jnp.where(hit, 1.0, 0.0)
            return acc + jnp.sum(ones.reshape(SEL // COUNT_ROWS, COUNT_ROWS, TQ), axis=0)
        acc = lax.fori_loop(0, n_sel, body, jnp.zeros((COUNT_ROWS, TQ), F32))
        return jnp.sum(acc, axis=0, keepdims=True)

    def count16(src_scr, pred):
        def body(slot, acc):
            ones = jnp.where(pred(src_scr[slot]), jnp.int16(1), jnp.int16(0))
            for r0 in range(0, SEL, COUNT_ROWS):
                acc = acc + ones[r0:r0 + COUNT_ROWS]
            return acc
        acc = lax.fori_loop(0, n_sel, body, jnp.zeros((COUNT_ROWS, TQ), jnp.int16))
        return jnp.sum(acc.astype(F32), axis=0, keepdims=True)

    kf = float(topk)

    def search16(src_scr, base):
        c0 = count16(src_scr, lambda x: x >= 0)
        t0 = jnp.where(base + c0 >= kf, 0, -HALF_BIAS).astype(jnp.int32)

        def step(it, t):
            cand = t + (jnp.int32(1) << (14 - it))
            cand16 = cand.astype(jnp.int16)
            c = count16(src_scr, lambda x: x >= cand16)
            return jnp.where(base + c >= kf, cand, t)

        return lax.fori_loop(0, 15, step, t0)

    thr_hi = search16(hi_scr, 0.0)
    thr_hi16 = thr_hi.astype(jnp.int16)

    def mask_lo(slot, carry):
        lo_scr[slot] = jnp.where(hi_scr[slot] == thr_hi16, lo_scr[slot], jnp.int16(-HALF_BIAS))
        return carry

    lax.fori_loop(0, n_sel, mask_lo, 0)
    cnt_gt_hi = count16(hi_scr, lambda x: x > thr_hi16)
    thr_lo = search16(lo_scr, cnt_gt_hi)
    thr = thr_hi * (2 * HALF_BIAS) + (thr_lo + HALF_BIAS)

    thr_lo16 = thr_lo.astype(jnp.int16)
    bucket = count16(hi_scr, lambda x: x == thr_hi16)
    ge_lo = jnp.where(thr_lo == -HALF_BIAS, bucket, count16(lo_scr, lambda x: x >= thr_lo16))
    cnt_ge = cnt_gt_hi + ge_lo
    need = kf - (cnt_gt_hi + count16(lo_scr, lambda x: x > thr_lo16))
    jmax_scr[...] = jnp.full((SUBLANES, TQ), seq_len, jnp.int32)

    @pl.when(jnp.max(jnp.where(thr == INT_MIN, 0.0, cnt_ge)) > kf)
    def _():
        tie_bits = (seq_len - 1).bit_length()

        def tie_step(it, jmax):
            cand = jmax + (jnp.int32(1) << (tie_bits - 1 - it))
            c = count(lambda key, slot: (key == thr) & (kpos_s + slot * SEL < cand))
            return jnp.where(c < need, cand, jmax)

        jmax = lax.fori_loop(0, tie_bits, tie_step, jnp.zeros((1, TQ), jnp.int32))
        jmax_scr[...] = jnp.broadcast_to(jmax, (SUBLANES, TQ))

    jmax = jnp.where(thr == INT_MIN, qpos_row, jmax_scr[0:1, :])

    n_slots_total = n_chunks * (KC // SEL)

    def bias_slot(slot):
        slot = jnp.minimum(slot, n_slots_total - 1)
        key = key_scr[slot]
        kp = kpos_s + slot * SEL
        sel = (key > thr) | ((key == thr) & (kp <= jmax))
        bias_scr[slot] = jnp.transpose(jnp.where(sel, 0.0, NEG_BIG))

    qv = q_ref[...]
    rows = (N_HEADS // N_KV_HEADS) * TQ
    qs = []
    for n in range(N_KV_HEADS):
        keep = lower if n == 0 else jnp.logical_not(lower)
        parts = [jnp.where(keep, qv[:, c * LANES:(c + 1) * LANES], zero_b) for c in range(D_ATTN // LANES)]
        qs.append(jnp.concatenate(parts, axis=0))

    def flash_update(first_slot, n_slots):
        width = n_slots * SEL
        k0 = pl.multiple_of(first_slot * SEL, SEL)
        kvc = kv_ref[pl.ds(k0, width), :]
        kch = kvc[:, 0:LANES]
        vch = kvc[:, LANES:2 * LANES]
        bias = jnp.concatenate([bias_scr[first_slot + part] for part in range(n_slots)], axis=1)
        bias4 = jnp.concatenate([bias] * (N_HEADS // N_KV_HEADS), axis=0)
        lane_k = lax.broadcasted_iota(jnp.int32, (width, LANES), 1)
        for n in range(N_KV_HEADS):
            m = m_scr[n]
            own = (lane_k < HEAD_DIM) if n == 0 else (lane_k >= HEAD_DIM)
            v_aug = jnp.where(own, vch, jnp.ones_like(vch))
            s = lax.dot_general(qs[n], kch, nt_dims, preferred_element_type=F32) + bias4
            row_max = jnp.broadcast_to(jnp.max(s, axis=-1, keepdims=True), (rows, LANES))
            m_new = jnp.maximum(m, row_max)
            p = jnp.exp2((s - jnp.concatenate([m_new] * (width // LANES), axis=1)).astype(BF16))
            acc_scr[n] = jnp.exp2(m - m_new) * acc_scr[n] + jnp.dot(p, v_aug, preferred_element_type=F32)
            m_scr[n] = m_new

    m_scr[...] = jnp.full(m_scr.shape, NEG_BIG, F32)
    acc_scr[...] = jnp.zeros(acc_scr.shape, F32)
    for part in range(FLASH_SLOTS):
        bias_slot(part)

    def flash_wide(j, carry):
        for part in range(FLASH_SLOTS):
            bias_slot((j + 1) * FLASH_SLOTS + part)
        flash_update(j * FLASH_SLOTS, FLASH_SLOTS)
        return carry

    lax.fori_loop(0, n_slots_total // FLASH_SLOTS, flash_wide, 0)
    for rem in range(1, FLASH_SLOTS):
        @pl.when(n_slots_total % FLASH_SLOTS == rem)
        def _(rem=rem):
            flash_update(n_slots_total - rem, rem)

    for c in range(D_ATTN // LANES):
        a0 = acc_scr[0, c * TQ:(c + 1) * TQ, :]
        a1 = acc_scr[1, c * TQ:(c + 1) * TQ, :]
        num = jnp.where(lower, a0, a1)
        den = pltpu.roll(jnp.where(lower, a1, a0), HEAD_DIM, 1)
        o_ref[:, c * LANES:(c + 1) * LANES] = (num / den).astype(BF16)


def _attn_call(q, qi, wi, kv, ki, batch, seq_len):
    n = q.shape[0]
    nq = seq_len // TQ
    topk = min(TOPK_MAX, seq_len // 4)
    qspec = lambda w: pl.BlockSpec((TQ, w), lambda b, i: (b * nq + i, 0))
    sspec = lambda w: pl.BlockSpec((seq_len, w), lambda b, i: (b, 0))
    wspec = pl.BlockSpec((WI_ROWS, TQ), lambda b, i: (0, b * nq + i))
    return pl.pallas_call(
        functools.partial(_attn_kernel, topk, seq_len),
        grid=(batch, nq),
        in_specs=[qspec(D_ATTN), qspec(IDX_HEADS * IDX_DIM), wspec, sspec(2 * LANES), sspec(LANES)],
        out_specs=qspec(D_ATTN),
        out_shape=jax.ShapeDtypeStruct((n, D_ATTN), BF16),
        scratch_shapes=[pltpu.VMEM((seq_len // SEL, SEL, TQ), jnp.int32),
                        pltpu.VMEM((seq_len // SEL, TQ, SEL), F32),
                        pltpu.VMEM((IDX_HEADS, TQ, LANES), BF16),
                        pltpu.VMEM((SUBLANES, TQ), jnp.int32),
                        pltpu.VMEM((seq_len // SEL, SEL, TQ), jnp.int16),
                        pltpu.VMEM((seq_len // SEL, SEL, TQ), jnp.int16),
                        pltpu.VMEM((N_KV_HEADS, (N_HEADS // N_KV_HEADS) * TQ, LANES), F32),
                        pltpu.VMEM((N_KV_HEADS, (N_HEADS // N_KV_HEADS) * TQ, LANES), F32)],
        compiler_params=_compiler_params(2),
        name="dsa_attn",
    )(q, qi, wi, kv, ki)


def _mix_kernel(alpha, h_ref, o_ref, ypre_ref, cpre_ref, wg_ref, bg_ref, wpo_ref, wco_ref, wao_ref,
                wo_ref, g_ref, b_ref, out_ref):
    h = h_ref[...]
    hb = h.astype(BF16)
    branches = ((ypre_ref, wpo_ref), (cpre_ref, wco_ref), (o_ref, wao_ref))
    merged = jnp.zeros((TM, D_MODEL), F32)
    for n, (x_ref, w_ref) in enumerate(branches):
        gate = jnp.dot(hb, wg_ref[:, n * D_MODEL:(n + 1) * D_MODEL], preferred_element_type=F32)
        gate = jax.nn.sigmoid(gate + bg_ref[:, n * D_MODEL:(n + 1) * D_MODEL])
        merged = merged + gate * jnp.dot(x_ref[...], w_ref[...], preferred_element_type=F32)
    mix = jnp.dot(merged.astype(BF16), wo_ref[...], preferred_element_type=F32)
    out_ref[...] = _layer_norm(alpha * h + mix, g_ref[...], b_ref[...])


def _mix_call(alpha, h, o, ypre, cpre, wg, bg, wpo, wco, wao, wo, g, b):
    n = h.shape[0]
    row_spec = lambda w: pl.BlockSpec((TM, w), lambda i: (i, 0))
    return pl.pallas_call(
        functools.partial(_mix_kernel, alpha),
        grid=(n // TM,),
        in_specs=[row_spec(D_MODEL), row_spec(D_ATTN), row_spec(D_POOL), row_spec(D_CONV)] + [_resident()] * 8,
        out_specs=row_spec(D_MODEL),
        out_shape=jax.ShapeDtypeStruct((n, D_MODEL), F32),
        compiler_params=_compiler_params(1),
        name="gated_merge",
    )(h, o, ypre, cpre, wg, bg, wpo, wco, wao, wo, g, b)


def _ffn_kernel(alpha, tiles_per_seq, h_ref, wu_ref, dw_ref, db_ref, wd_ref, g_ref, b_ref, out_ref,
                carry_ref, act_ref):
    i = pl.program_id(0)
    first = lax.rem(i, tiles_per_seq) == 0
    h = h_ref[...]
    hb = h.astype(BF16)

    @pl.when(first)
    def _():
        carry_ref[...] = jnp.zeros_like(carry_ref)

    def conv3(u, lo):
        cols = slice(lo, lo + FC)
        prev = carry_ref[:, cols]
        carry_ref[:, cols] = u[TM - FFN_HALO:, :]
        ext = jnp.concatenate([prev, u], axis=0)
        y = (dw_ref[2:3, cols] * u + dw_ref[1:2, cols] * pltpu.roll(ext, 1, 0)[FFN_HALO:]
             + dw_ref[0:1, cols] * pltpu.roll(ext, 2, 0)[FFN_HALO:])
        return y + db_ref[:, cols]

    def up(c):
        return tuple(jnp.dot(hb, wu_ref[:, lo:lo + FC], preferred_element_type=F32)
                     for lo in (c * FC, D_FF + c * FC))

    nc = D_FF // FC
    u_next = up(0)
    for c in range(nc):
        ug, uv = u_next
        if c + 1 < nc:
            u_next = up(c + 1)
        gate = conv3(ug, c * FC)
        val = conv3(uv, D_FF + c * FC)
        act_ref[:, c * FC:(c + 1) * FC] = (gate * jax.nn.sigmoid(gate) * val).astype(BF16)
    ffn = jnp.dot(act_ref[...], wd_ref[...], preferred_element_type=F32)
    out_ref[...] = _layer_norm(alpha * h + ffn, g_ref[...], b_ref[...])


def _ffn_call(alpha, h, wu, dw, db, wd, g, b, seq_len):
    n = h.shape[0]
    row_spec = pl.BlockSpec((TM, D_MODEL), lambda i: (i, 0))
    return pl.pallas_call(
        functools.partial(_ffn_kernel, alpha, seq_len // TM),
        grid=(n // TM,),
        in_specs=[row_spec] + [_resident()] * 6,
        out_specs=row_spec,
        out_shape=jax.ShapeDtypeStruct((n, D_MODEL), F32),
        scratch_shapes=[pltpu.VMEM((FFN_HALO, 2 * D_FF), F32), pltpu.VMEM((TM, D_FF), BF16)],
        compiler_params=_compiler_params(1),
        name="conv_ffn",
    )(h, wu, dw, db, wd, g, b)


def _head_pair_perm():
    group = N_HEADS // N_KV_HEADS
    cols = []
    for j in range(group):
        cols.extend(range(j * HEAD_DIM, (j + 1) * HEAD_DIM))
        cols.extend(range((group + j) * HEAD_DIM, (group + j + 1) * HEAD_DIM))
    return np.asarray(cols)


def _prep_in_weights(w_in, b_in):
    sizes = (D_POOL, 2 * D_CONV, D_ATTN, D_KV, D_KV, IDX_HEADS * IDX_DIM, IDX_DIM, IDX_HEADS, N_BRANCH * D_MODEL)
    offs = np.concatenate([[0], np.cumsum(sizes)])
    seg = lambda a, k: a[..., offs[k]:offs[k + 1]]
    perm = _head_pair_perm()

    def relayout(a):
        pad = jnp.zeros(a.shape[:-1] + (LANES - IDX_HEADS,), a.dtype)
        return jnp.concatenate([seg(a, 0), seg(a, 1), seg(a, 2)[..., perm], seg(a, 3), seg(a, 4), seg(a, 5),
                                seg(a, 6), seg(a, 6), seg(a, 7), pad], axis=-1)

    wa = relayout(w_in).astype(BF16)
    ba = relayout(b_in[None, :])
    wg = seg(w_in, 8).astype(BF16)
    bg = seg(b_in[None, :], 8)
    return wa, ba, wg, bg


def _block_diag(pool_w):
    out = jnp.zeros((D_POOL, D_POOL), pool_w.dtype)
    for g in range(POOL_GROUPS):
        lo = g * POOL_GROUP_DIM
        out = out.at[lo:lo + POOL_GROUP_DIM, lo:lo + POOL_GROUP_DIM].set(pool_w[g])
    return out


def kernel(x, positions, ln_in_g, ln_in_b, w_in, b_in, pool_w, pool_scale, w_pool_out, conv_dw_w, conv_dw_b,
           conv_ln_g, conv_ln_b, w_conv_out, w_attn_out, w_o, ln1_g, ln1_b, w_up, ffn_dw_w, ffn_dw_b, w_down,
           ln2_g, ln2_b):
    batch, seq_len, d_model = x.shape
    depth = w_in.shape[0]
    assert d_model == D_MODEL and seq_len % TM == 0 and seq_len % KC == 0 and KC % TQ == 0
    assert KC % SEL == 0 and SEL % COUNT_ROWS == 0
    n = batch * seq_len
    alpha = float((2 * depth) ** 0.25)
    perm = _head_pair_perm()
    row = lambda a: a.reshape(1, -1)

    cos_t, sin_t = _rope_tables(positions)
    h = x.reshape(n, D_MODEL)
    for l in range(depth):
        wa, ba, wg, bg = _prep_in_weights(w_in[l], b_in[l])
        outs = _in_call(
            h, l == 0, row(ln_in_g), row(ln_in_b), cos_t, sin_t, wa, ba,
            _block_diag(pool_w[l]).astype(BF16), row(pool_scale[l]),
            conv_dw_w[l], row(conv_dw_b[l]), row(conv_ln_g[l]), row(conv_ln_b[l]), seq_len)
        q, kv, qi, ki, wi, ypre, cpre = outs[:7]
        if l == 0:
            h = outs[7]
        o = _attn_call(q, qi, wi, kv, ki, batch, seq_len)
        h = _mix_call(alpha, h, o, ypre, cpre, wg, bg, w_pool_out[l].astype(BF16), w_conv_out[l].astype(BF16),
                      w_attn_out[l][perm].astype(BF16), w_o[l].astype(BF16), row(ln1_g[l]), row(ln1_b[l]))
        h = _ffn_call(alpha, h, w_up[l].astype(BF16), ffn_dw_w[l], row(ffn_dw_b[l]), w_down[l].astype(BF16),
                      row(ln2_g[l]), row(ln2_b[l]), seq_len)
    return h.reshape(batch, seq_len, D_MODEL)
```
